```python
import math
import jax, jax.numpy as jnp
from jax import lax
import numpy as np

D_MODEL = 1024
BATCH = 2
SEQ = 8192
DEPTH = 2

HEAD_DIM = 64
N_Q_HEADS = 8
N_KV_HEADS = 2
GROUP = N_Q_HEADS // N_KV_HEADS
WINDOW = 128
ATTN_BLOCK = 128
ROPE_THETA = 500000.0
ROT_DIM = HEAD_DIM // 4
ATTN_WIDTH = N_Q_HEADS * HEAD_DIM
KV_WIDTH = N_KV_HEADS * HEAD_DIM

MLSTM_HEADS = 4
MLSTM_HEAD_DIM = D_MODEL // 8
MLSTM_WIDTH = MLSTM_HEADS * MLSTM_HEAD_DIM
MLSTM_CHUNK = 128

CONV_WIDTH = D_MODEL // 2
CONV_K = 3

N_BRANCH = 3
BRANCH_WIDTH = 512
D_FF = 2816
LN_EPS = 1e-5
ALPHA = (2.0 * DEPTH) ** 0.25
BETA = (8.0 * DEPTH) ** -0.25
NEG_INF = -1e30

SIZES = (ATTN_WIDTH, KV_WIDTH, KV_WIDTH,
         MLSTM_WIDTH, MLSTM_WIDTH, MLSTM_WIDTH, MLSTM_WIDTH, MLSTM_HEADS, MLSTM_HEADS,
         CONV_WIDTH, CONV_WIDTH, CONV_WIDTH,
         N_BRANCH * D_MODEL)
P_IN = sum(SIZES)

kernel_name = "hybrid_swa_mlstm_shortconv_macaron_deepnorm"


def _split_points():
    pts, acc = [], 0
    for s in SIZES[:-1]:
        acc += s
        pts.append(acc)
    return pts


def layer_norm(x, w, b):
    xf = x.astype(jnp.float32)
    mu = jnp.mean(xf, axis=-1, keepdims=True)
    xc = xf - mu
    var = jnp.mean(xc * xc, axis=-1, keepdims=True)
    y = xc * lax.rsqrt(var + LN_EPS) * w.astype(jnp.float32) + b.astype(jnp.float32)
    return y.astype(x.dtype)


def swiglu(x, w_gu, w_down):
    g, u = jnp.split(x @ w_gu, 2, axis=-1)
    return (jax.nn.silu(g) * u) @ w_down


def partial_rope(t, positions):
    half = ROT_DIM // 2
    inv_freq = ROPE_THETA ** (-jnp.arange(0, ROT_DIM, 2, dtype=jnp.float32) / ROT_DIM)
    ang = positions.astype(jnp.float32)[..., None] * inv_freq
    cos = jnp.cos(ang)[:, :, None, :].astype(t.dtype)
    sin = jnp.sin(ang)[:, :, None, :].astype(t.dtype)
    t1, t2, rest = t[..., :half], t[..., half:ROT_DIM], t[..., ROT_DIM:]
    return jnp.concatenate([t1 * cos - t2 * sin, t2 * cos + t1 * sin, rest], axis=-1)


def sliding_window_attention(q, k, v, sinks):
    B, S = q.shape[0], q.shape[1]
    L = ATTN_BLOCK
    nb = S // L
    qb = q.reshape(B, nb, L, N_KV_HEADS, GROUP, HEAD_DIM)
    kb = k.reshape(B, nb, L, N_KV_HEADS, HEAD_DIM)
    vb = v.reshape(B, nb, L, N_KV_HEADS, HEAD_DIM)
    zk = jnp.zeros_like(kb[:, :1])
    kk = jnp.concatenate([jnp.concatenate([zk, kb[:, :-1]], axis=1), kb], axis=2)
    vv = jnp.concatenate([jnp.concatenate([zk, vb[:, :-1]], axis=1), vb], axis=2)
    s = jnp.einsum('bnqhgd,bnkhd->bnhgqk', qb, kk,
                   preferred_element_type=jnp.float32) * (HEAD_DIM ** -0.5)
    qi = jnp.arange(L)[:, None]
    kj = jnp.arange(2 * L)[None, :]
    rel = qi + L - kj
    band = (rel >= 0) & (rel < WINDOW)
    valid = (jnp.arange(nb)[:, None, None] > 0) | (kj >= L)[None]
    mask = band[None] & valid
    s = jnp.where(mask[None, :, None, None], s, NEG_INF)
    sk = sinks.astype(jnp.float32).reshape(N_KV_HEADS, GROUP)[None, None, :, :, None, None]
    m = jnp.maximum(jnp.max(s, axis=-1, keepdims=True), sk)
    p = jnp.exp(s - m)
    p = p / (jnp.sum(p, axis=-1, keepdims=True) + jnp.exp(sk - m))
    o = jnp.einsum('bnhgqk,bnkhd->bnqhgd', p.astype(v.dtype), vv)
    return o.reshape(B, S, N_Q_HEADS * HEAD_DIM)


def mlstm_chunkwise(q, k, v, i_pre, f_pre):
    B, S, H, d = q.shape
    L = MLSTM_CHUNK
    nc = S // L
    f32 = jnp.float32

    def chunk4(t):
        return t.astype(f32).reshape(B, nc, L, H, d).transpose(1, 0, 3, 2, 4)

    def chunk3(t):
        return t.astype(f32).reshape(B, nc, L, H).transpose(1, 0, 3, 2)

    qc = chunk4(q)
    kc = chunk4(k) * (d ** -0.5)
    vc = chunk4(v)
    ic = chunk3(i_pre)
    fc = chunk3(jax.nn.log_sigmoid(f_pre.astype(f32)))
    causal = jnp.tril(jnp.ones((L, L), dtype=bool))

    def step(carry, xs):
        C, n, m = carry
        qt, kt, vt, it, ft = xs
        b = jnp.cumsum(ft, axis=-1)
        logw = b[..., :, None] - b[..., None, :] + it[..., None, :]
        logw = jnp.where(causal, logw, NEG_INF)
        inter = b + m[..., None]
        m_t = jnp.maximum(inter, jnp.max(logw, axis=-1))
        w_inter = jnp.exp(inter - m_t)
        scores = jnp.einsum('bhtd,bhsd->bhts', qt, kt) * jnp.exp(logw - m_t[..., None])
        num = w_inter[..., None] * jnp.einsum('bhtk,bhvk->bhtv', qt, C) + \
            jnp.einsum('bhts,bhsv->bhtv', scores, vt)
        den = w_inter * jnp.einsum('bhtk,bhk->bht', qt, n) + jnp.sum(scores, axis=-1)
        h = num / jnp.maximum(jnp.abs(den), jnp.exp(-m_t))[..., None]
        m_new = m_t[..., -1]
        bL = b[..., -1]
        decay = jnp.exp(bL + m - m_new)
        ws = jnp.exp(bL[..., None] - b + it - m_new[..., None])
        C_new = decay[..., None, None] * C + jnp.einsum('bhs,bhsv,bhsk->bhvk', ws, vt, kt)
        n_new = decay[..., None] * n + jnp.einsum('bhs,bhsk->bhk', ws, kt)
        return (C_new, n_new, m_new), h

    init = (jnp.zeros((B, H, d, d), f32), jnp.zeros((B, H, d), f32), jnp.zeros((B, H), f32))
    _, hs = lax.scan(step, init, (qc, kc, vc, ic, fc))
    return hs.transpose(1, 0, 3, 2, 4).reshape(B, S, H, d)


def short_gated_conv(xin, bg, cg, w):
    S = xin.shape[1]
    u = cg * xin
    up = jnp.pad(u, ((0, 0), (CONV_K - 1, 0), (0, 0)))
    y = w[0] * up[:, 0:S]
    for j in range(1, CONV_K):
        y = y + w[j] * up[:, j:j + S]
    return bg * y


def hybrid_mixer(x, positions, w_in, b_in, sinks, mlstm_norm_w, conv_w, w_branch, w_out):
    B, S, _ = x.shape
    proj = x @ w_in + b_in
    (aq, ak, av, mq, mk, mv, mo, mi, mf, cx, cb, cc, gates) = jnp.split(proj, _split_points(), axis=-1)
    aq = partial_rope(aq.reshape(B, S, N_Q_HEADS, HEAD_DIM), positions)
    ak = partial_rope(ak.reshape(B, S, N_KV_HEADS, HEAD_DIM), positions)
    av = av.reshape(B, S, N_KV_HEADS, HEAD_DIM)
    y_attn = sliding_window_attention(aq, ak, av, sinks)
    h = mlstm_chunkwise(mq.reshape(B, S, MLSTM_HEADS, MLSTM_HEAD_DIM),
                        mk.reshape(B, S, MLSTM_HEADS, MLSTM_HEAD_DIM),
                        mv.reshape(B, S, MLSTM_HEADS, MLSTM_HEAD_DIM), mi, mf)
    hm = jnp.mean(h, axis=-1, keepdims=True)
    hc = h - hm
    h = hc * lax.rsqrt(jnp.mean(hc * hc, axis=-1, keepdims=True) + LN_EPS)
    h = h.reshape(B, S, MLSTM_WIDTH) * mlstm_norm_w.astype(jnp.float32)
    y_mlstm = h.astype(x.dtype) * jax.nn.sigmoid(mo)
    y_conv = short_gated_conv(cx, cb, cc, conv_w)
    g = jax.nn.sigmoid(gates).reshape(B, S, N_BRANCH, D_MODEL)
    merged = (g[:, :, 0] * (y_attn @ w_branch[0]) +
              g[:, :, 1] * (y_mlstm @ w_branch[1]) +
              g[:, :, 2] * (y_conv @ w_branch[2]))
    return merged @ w_out


def setup_inputs(seed: int = 0) -> dict:
    key = jax.random.key(seed)
    ks = jax.random.split(key, 16)
    nrm = jax.random.normal
    x = nrm(ks[0], (BATCH, SEQ, D_MODEL), jnp.float32)
    positions = (jnp.arange(SEQ, dtype=jnp.int32)[None, :] +
                 jax.random.randint(ks[1], (BATCH, 1), 0, 1024, dtype=jnp.int32))
    w_in = nrm(ks[2], (DEPTH, D_MODEL, P_IN), jnp.float32) * D_MODEL ** -0.5
    b_in = nrm(ks[3], (DEPTH, P_IN), jnp.float32) * 0.02
    f_off = sum(SIZES[:8])
    b_in = b_in.at[:, f_off:f_off + MLSTM_HEADS].add(jnp.linspace(3.0, 6.0, MLSTM_HEADS, dtype=jnp.float32))
    attn_sinks = nrm(ks[4], (DEPTH, N_Q_HEADS), jnp.float32) * 0.5
    mlstm_norm_w = 1.0 + 0.02 * nrm(ks[5], (DEPTH, MLSTM_WIDTH), jnp.float32)
    conv_w = nrm(ks[6], (DEPTH, CONV_K, CONV_WIDTH), jnp.float32) * CONV_K ** -0.5
    w_branch = nrm(ks[7], (DEPTH, N_BRANCH, BRANCH_WIDTH, D_MODEL), jnp.float32) * (BRANCH_WIDTH ** -0.5 * BETA)
    w_out = nrm(ks[8], (DEPTH, D_MODEL, D_MODEL), jnp.float32) * (D_MODEL ** -0.5 * BETA)
    ffn1_w_gu = nrm(ks[9], (DEPTH, D_MODEL, 2 * D_FF), jnp.float32) * D_MODEL ** -0.5
    ffn1_w_down = nrm(ks[10], (DEPTH, D_FF, D_MODEL), jnp.float32) * (D_FF ** -0.5 * BETA)
    ffn2_w_gu = nrm(ks[11], (DEPTH, D_MODEL, 2 * D_FF), jnp.float32) * D_MODEL ** -0.5
    ffn2_w_down = nrm(ks[12], (DEPTH, D_FF, D_MODEL), jnp.float32) * (D_FF ** -0.5 * BETA)
    ln_w = 1.0 + 0.02 * nrm(ks[13], (DEPTH, 3, D_MODEL), jnp.float32)
    ln_b = 0.02 * nrm(ks[14], (DEPTH, 3, D_MODEL), jnp.float32)
    return {"x": x, "positions": positions, "w_in": w_in, "b_in": b_in,
            "attn_sinks": attn_sinks, "mlstm_norm_w": mlstm_norm_w, "conv_w": conv_w,
            "w_branch": w_branch, "w_out": w_out,
            "ffn1_w_gu": ffn1_w_gu, "ffn1_w_down": ffn1_w_down,
            "ffn2_w_gu": ffn2_w_gu, "ffn2_w_down": ffn2_w_down,
            "ln_w": ln_w, "ln_b": ln_b}


def reference(x, positions, w_in, b_in, attn_sinks, mlstm_norm_w, conv_w, w_branch, w_out,
              ffn1_w_gu, ffn1_w_down, ffn2_w_gu, ffn2_w_down, ln_w, ln_b):
    for l in range(DEPTH):
        x = layer_norm(ALPHA * x + 0.5 * swiglu(x, ffn1_w_gu[l], ffn1_w_down[l]), ln_w[l, 0], ln_b[l, 0])
        mix = hybrid_mixer(x, positions, w_in[l], b_in[l], attn_sinks[l], mlstm_norm_w[l],
                           conv_w[l], w_branch[l], w_out[l])
        x = layer_norm(ALPHA * x + mix, ln_w[l, 1], ln_b[l, 1])
        x = layer_norm(ALPHA * x + 0.5 * swiglu(x, ffn2_w_gu[l], ffn2_w_down[l]), ln_w[l, 2], ln_b[l, 2])
    return x
```

```python
import functools
import math

import numpy as np
import jax
import jax.numpy as jnp
from jax import lax
from jax.experimental import pallas as pl
from jax.experimental.pallas import tpu as pltpu

D_MODEL = 1024
DEPTH = 2

HEAD_DIM = 64
N_Q_HEADS = 8
N_KV_HEADS = 2
GROUP = N_Q_HEADS // N_KV_HEADS
WINDOW = 128
ATTN_BLOCK = 128
ROPE_THETA = 500000.0
ROT_DIM = HEAD_DIM // 4
ATTN_WIDTH = N_Q_HEADS * HEAD_DIM
KV_WIDTH = N_KV_HEADS * HEAD_DIM

MLSTM_HEADS = 4
MLSTM_HEAD_DIM = D_MODEL // 8
MLSTM_WIDTH = MLSTM_HEADS * MLSTM_HEAD_DIM
MLSTM_CHUNK = 128

CONV_WIDTH = D_MODEL // 2
CONV_K = 3

N_BRANCH = 3
BRANCH_WIDTH = 512
D_FF = 2816
LN_EPS = 1e-5
ALPHA = (2.0 * DEPTH) ** 0.25
NEG_INF = -1e30

SIZES = (ATTN_WIDTH, KV_WIDTH, KV_WIDTH,
         MLSTM_WIDTH, MLSTM_WIDTH, MLSTM_WIDTH, MLSTM_WIDTH, MLSTM_HEADS, MLSTM_HEADS,
         CONV_WIDTH, CONV_WIDTH, CONV_WIDTH,
         N_BRANCH * D_MODEL)

LANES = 128
SUBLANES = 8
VMEM_LIMIT_BYTES = 56 * 1024 * 1024

OFF_AQ = 0
OFF_AK = OFF_AQ + ATTN_WIDTH
OFF_AV = OFF_AK + KV_WIDTH
OFF_M = OFF_AV + KV_WIDTH
OFF_C = OFF_M + 4 * MLSTM_WIDTH
OFF_G = OFF_C + 3 * CONV_WIDTH
OFF_IF = OFF_G + N_BRANCH * D_MODEL
P_PACKED = OFF_IF + 2 * LANES

Q_HEAD_ORDER = (0, 4, 1, 5, 2, 6, 3, 7)

BF16 = jnp.bfloat16
F32 = jnp.float32


def _dot(a, b):
    return jnp.dot(a, b, preferred_element_type=F32)


def _dot_nt(a, b):
    return lax.dot_general(a, b, (((1,), (1,)), ((), ())), preferred_element_type=F32)


def _dot_tn(a, b):
    return lax.dot_general(a, b, (((0,), (0,)), ((), ())), preferred_element_type=F32)


def _layer_norm_rows(y, w, b):
    mu = jnp.mean(y, axis=-1, keepdims=True)
    yc = y - mu
    var = jnp.mean(yc * yc, axis=-1, keepdims=True)
    return yc * lax.rsqrt(var + LN_EPS) * w + b


def _sigmoid(x):
    return 1.0 / (1.0 + jnp.exp(-x))


def _log_sigmoid(x):
    return jnp.minimum(x, 0.0) - jnp.log1p(jnp.exp(-jnp.abs(x)))


def _ffn_kernel(x_ref, wg_ref, wu_ref, wd_ref, lnw_ref, lnb_ref, o_ref, xb_ref, acc_ref):
    j = pl.program_id(1)
    last = pl.num_programs(1) - 1

    @pl.when(j == 0)
    def _cast():
        xb_ref[...] = x_ref[...].astype(BF16)

    xb = xb_ref[...]
    g = _dot(xb, wg_ref[0])
    u = _dot(xb, wu_ref[0])
    a = (g * _sigmoid(g) * u).astype(BF16)
    part = _dot(a, wd_ref[0])

    @pl.when(j == 0)
    def _first():
        acc_ref[...] = part

    @pl.when(j > 0)
    def _rest():
        acc_ref[...] += part

    @pl.when(j == last)
    def _finish():
        y = ALPHA * x_ref[...] + 0.5 * acc_ref[...]
        o_ref[...] = _layer_norm_rows(y, lnw_ref[...], lnb_ref[...])


def _ffn_tiles(n_tokens):
    tm = 1024 if n_tokens % 1024 == 0 else 128
    tf = 256
    return tm, tf


def _ffn_layer(x2d, w_gu, w_down, ln_w, ln_b):
    n, d = x2d.shape
    d_ff = w_down.shape[0]
    tm, tf = _ffn_tiles(n)
    nf = d_ff // tf
    wg = w_gu[:, :d_ff].astype(BF16).reshape(d, nf, tf).transpose(1, 0, 2)
    wu = w_gu[:, d_ff:].astype(BF16).reshape(d, nf, tf).transpose(1, 0, 2)
    wd = w_down.astype(BF16).reshape(nf, tf, d)
    return pl.pallas_call(
        _ffn_kernel,
        out_shape=jax.ShapeDtypeStruct((n, d), F32),
        grid=(n // tm, nf),
        in_specs=[
            pl.BlockSpec((tm, d), lambda i, j: (i, 0)),
            pl.BlockSpec((1, d, tf), lambda i, j: (j, 0, 0)),
            pl.BlockSpec((1, d, tf), lambda i, j: (j, 0, 0)),
            pl.BlockSpec((1, tf, d), lambda i, j: (j, 0, 0)),
            pl.BlockSpec((1, d), lambda i, j: (0, 0)),
            pl.BlockSpec((1, d), lambda i, j: (0, 0)),
        ],
        out_specs=pl.BlockSpec((tm, d), lambda i, j: (i, 0)),
        scratch_shapes=[pltpu.VMEM((tm, d), BF16), pltpu.VMEM((tm, d), F32)],
        compiler_params=pltpu.CompilerParams(
            dimension_semantics=("arbitrary", "arbitrary"),
            vmem_limit_bytes=VMEM_LIMIT_BYTES),
        name="swiglu_ln",
    )(x2d, wg, wu, wd, ln_w.reshape(1, d), ln_b.reshape(1, d))


def _rope_tables(pos_row, tab):
    pos_t = jnp.broadcast_to(pos_row.astype(F32), (ATTN_BLOCK, LANES)).T
    ang = pos_t * tab[0:1, :]
    cos_f = jnp.cos(ang)
    sin_f = jnp.sin(ang)
    return cos_f, sin_f * tab[1:2, :], sin_f * tab[2:3, :]


def _rope(t, cos_f, sin_a, sin_b):
    half = ROT_DIM // 2
    up = pltpu.roll(t, LANES - half, axis=1)
    down = pltpu.roll(t, half, axis=1)
    return t * cos_f + up * sin_a + down * sin_b


def _attention_block(q_blk, k_cur, v_cur, k_prev, v_prev, thr, sinks_ref):
    L = ATTN_BLOCK
    lane = lax.broadcasted_iota(jnp.int32, (1, LANES), 1)
    half0 = lane < HEAD_DIM
    zero = jnp.zeros((), BF16)

    def split(prev, cur):
        both = jnp.concatenate([prev, cur], axis=0)
        return jnp.concatenate([jnp.where(half0, both, zero),
                                jnp.where(half0, zero, both)], axis=0)

    kk = split(k_prev, k_cur)
    vv = split(v_prev, v_cur)

    qi = lax.broadcasted_iota(jnp.int32, (L, 4 * L), 0)
    kj = lax.broadcasted_iota(jnp.int32, (L, 4 * L), 1) & (2 * L - 1)
    rel = qi + L - kj
    mask = (rel >= 0) & (rel < WINDOW) & (kj >= thr)

    outs = []
    for c in range(N_Q_HEADS // 2):
        qc = q_blk[:, c * LANES:(c + 1) * LANES].astype(BF16)
        s = _dot_nt(qc, kk)
        s = jnp.where(mask, s, NEG_INF)
        ps, inv = [], []
        for g in range(N_KV_HEADS):
            sink = sinks_ref[Q_HEAD_ORDER[2 * c + g]]
            sg = s[:, g * 2 * L:(g + 1) * 2 * L]
            m = jnp.maximum(jnp.max(sg, axis=-1, keepdims=True), sink)
            p = jnp.exp(sg - m)
            den = jnp.sum(p, axis=-1, keepdims=True) + jnp.exp(sink - m)
            ps.append(p.astype(BF16))
            inv.append(1.0 / den)
        o = _dot(jnp.concatenate(ps, axis=1), vv)
        outs.append(o * jnp.where(half0, inv[0], inv[1]))
    return outs


def _mlstm_chunk(mproj, gif, s_ref, m_ref, nw_ref, row0):
    L = MLSTM_CHUNK
    dh = MLSTM_HEAD_DIM
    rows = slice(row0, row0 + L)
    lane = lax.broadcasted_iota(jnp.int32, (1, LANES), 1)
    gi = gif[rows, 0:LANES]
    gf = gif[rows, LANES:2 * LANES]
    lf = jnp.where(lane < 2 * MLSTM_HEADS, _log_sigmoid(gf), 0.0)

    r_i = lax.broadcasted_iota(jnp.int32, (L, L), 0)
    c_i = lax.broadcasted_iota(jnp.int32, (L, L), 1)
    causal = r_i >= c_i
    tril = jnp.where(causal, 1.0, 0.0).astype(BF16)
    hi = lf.astype(BF16)
    r1 = lf - hi.astype(F32)
    mid = r1.astype(BF16)
    lo = (r1 - mid.astype(F32)).astype(BF16)
    bsum = _dot(tril, hi) + _dot(tril, mid) + _dot(tril, lo)
    t_col = jnp.where(lane < MLSTM_HEADS, gi - bsum, bsum)
    t_row = t_col.T

    ones_col = jnp.where(lane == 0, 1.0, 0.0).astype(BF16)
    ones_tile = jnp.broadcast_to(ones_col, (L, LANES))
    scale = dh ** -0.5

    outs = []
    for h in range(MLSTM_HEADS):
        q = mproj[rows, h * dh:(h + 1) * dh].astype(BF16)
        k = mproj[rows, MLSTM_WIDTH + h * dh:MLSTM_WIDTH + (h + 1) * dh] * scale
        v = mproj[rows, 2 * MLSTM_WIDTH + h * dh:2 * MLSTM_WIDTH + (h + 1) * dh].astype(BF16)
        og = mproj[rows, 3 * MLSTM_WIDTH + h * dh:3 * MLSTM_WIDTH + (h + 1) * dh]
        b_col = t_col[:, MLSTM_HEADS + h:MLSTM_HEADS + h + 1]
        d_col = t_col[:, h:h + 1]
        d_row = t_row[h:h + 1, :]
        m_prev = m_ref[h][0:1, 0:1]
        state = s_ref[h]

        logw = jnp.where(causal, b_col + d_row, NEG_INF)
        inter = b_col + m_prev
        m_t = jnp.maximum(inter, jnp.max(logw, axis=-1, keepdims=True))
        w_inter = jnp.exp(inter - m_t)
        scores = _dot_nt(q, k.astype(BF16)) * jnp.exp(logw - m_t)
        row_sum = jnp.sum(scores, axis=-1, keepdims=True)
        from_state = _dot(q, state.astype(BF16))
        num = w_inter * from_state[:, 0:dh] + _dot(scores.astype(BF16), v)
        den = w_inter * from_state[:, dh:dh + 1] + row_sum
        hh = num / jnp.maximum(jnp.abs(den), jnp.exp(-m_t))

        m_new = m_t[L - 1:L, :]
        b_last = b_col[L - 1:L, :]
        decay = jnp.exp(b_last + m_prev - m_new)
        ws = jnp.exp(b_last + d_col - m_new)
        kw = (k * ws).astype(BF16)
        v_ext = jnp.concatenate([v, ones_tile], axis=1)
        s_ref[h] = decay * state + _dot_tn(kw, v_ext)
        m_ref[h] = jnp.broadcast_to(m_new, (SUBLANES, LANES))

        mu = jnp.mean(hh, axis=-1, keepdims=True)
        hc = hh - mu
        hn = hc * lax.rsqrt(jnp.mean(hc * hc, axis=-1, keepdims=True) + LN_EPS)
        outs.append(hn * nw_ref[:, h * dh:(h + 1) * dh] * _sigmoid(og))
    return outs


def _mixer_kernel(sinks_ref, x_ref, pos_ref, tab_ref, win_ref, bin_ref, nw_ref, cw_ref,
                  wb_ref, wo_ref, lnw_ref, lnb_ref, o_ref,
                  kprev_ref, vprev_ref, s_ref, m_ref, conv_ref, y_ref, *, tm):
    s_idx = pl.program_id(1)
    L = ATTN_BLOCK
    nblk = tm // L

    @pl.when(s_idx == 0)
    def _reset():
        kprev_ref[...] = jnp.zeros_like(kprev_ref)
        vprev_ref[...] = jnp.zeros_like(vprev_ref)
        s_ref[...] = jnp.zeros_like(s_ref)
        m_ref[...] = jnp.zeros_like(m_ref)
        conv_ref[0:SUBLANES, :] = jnp.zeros((SUBLANES, CONV_WIDTH), F32)

    x = x_ref[0]
    xb = x.astype(BF16)

    def proj(lo, hi):
        return _dot(xb, win_ref[:, lo:hi]) + bin_ref[:, lo:hi]

    qkv = proj(OFF_AQ, OFF_M)
    tab = tab_ref[...]
    for j in range(nblk):
        rows = slice(j * L, (j + 1) * L)
        cos_f, sin_a, sin_b = _rope_tables(pos_ref[0, j:j + 1, :], tab)
        q_blk = jnp.concatenate(
            [_rope(qkv[rows, c * LANES:(c + 1) * LANES], cos_f, sin_a, sin_b) * (HEAD_DIM ** -0.5)
             for c in range(N_Q_HEADS // 2)], axis=1)
        k_cur = _rope(qkv[rows, OFF_AK:OFF_AK + LANES], cos_f, sin_a, sin_b).astype(BF16)
        v_cur = qkv[rows, OFF_AV:OFF_AV + LANES].astype(BF16)
        if j == 0:
            k_prev, v_prev = kprev_ref[...], vprev_ref[...]
            thr = jnp.where(s_idx == 0, L, 0)
        else:
            thr = 0
        outs = _attention_block(q_blk, k_cur, v_cur, k_prev, v_prev, thr, sinks_ref)
        for c, o in enumerate(outs):
            y_ref[rows, c * LANES:(c + 1) * LANES] = o.astype(BF16)
        k_prev, v_prev = k_cur, v_cur
    kprev_ref[...] = k_prev
    vprev_ref[...] = v_prev

    mproj = proj(OFF_M, OFF_C)
    gif = proj(OFF_IF, P_PACKED)
    for j in range(nblk):
        outs = _mlstm_chunk(mproj, gif, s_ref, m_ref, nw_ref, j * L)
        for h, o in enumerate(outs):
            y_ref[j * L:(j + 1) * L,
                  BRANCH_WIDTH + h * MLSTM_HEAD_DIM:BRANCH_WIDTH + (h + 1) * MLSTM_HEAD_DIM] = o.astype(BF16)

    cproj = proj(OFF_C, OFF_G)
    u = cproj[:, 2 * CONV_WIDTH:] * cproj[:, 0:CONV_WIDTH]
    conv_ref[SUBLANES:SUBLANES + tm, :] = u
    u1 = conv_ref[pl.ds(SUBLANES - 1, tm), :]
    u2 = conv_ref[pl.ds(SUBLANES - 2, tm), :]
    yc = cw_ref[0:1, :] * u2 + cw_ref[1:2, :] * u1 + cw_ref[2:3, :] * u
    y_ref[:, 2 * BRANCH_WIDTH:] = (cproj[:, CONV_WIDTH:2 * CONV_WIDTH] * yc).astype(BF16)
    conv_ref[0:SUBLANES, :] = conv_ref[tm:tm + SUBLANES, :]

    merged = None
    for b in range(N_BRANCH):
        gate = _sigmoid(proj(OFF_G + b * D_MODEL, OFF_G + (b + 1) * D_MODEL))
        term = gate * _dot(y_ref[:, b * BRANCH_WIDTH:(b + 1) * BRANCH_WIDTH], wb_ref[b])
        merged = term if merged is None else merged + term
    mix = _dot(merged.astype(BF16), wo_ref[...])
    y = ALPHA * x + mix
    o_ref[0] = _layer_norm_rows(y, lnw_ref[...], lnb_ref[...])


def _rope_table():
    lane = np.arange(LANES)
    l64 = lane % HEAD_DIM
    half = ROT_DIM // 2
    inv_freq = ROPE_THETA ** (-np.arange(0, ROT_DIM, 2, dtype=np.float64) / ROT_DIM)
    tab = np.zeros((SUBLANES, LANES), np.float32)
    tab[0] = np.where(l64 < ROT_DIM, inv_freq[l64 % half], 0.0)
    tab[1] = np.where(l64 < half, -1.0, 0.0)
    tab[2] = np.where((l64 >= half) & (l64 < ROT_DIM), 1.0, 0.0)
    return jnp.asarray(tab)


def _pack_in_proj(w_in, b_in):
    def cols(w):
        pts = np.cumsum(SIZES)[:-1].tolist()
        (aq, ak, av, mq, mk, mv, mo, mi, mf, cx, cb, cc, gates) = jnp.split(w, pts, axis=-1)
        lead = w.shape[:-1]
        aq = aq.reshape(lead + (N_Q_HEADS, HEAD_DIM))[..., Q_HEAD_ORDER, :].reshape(lead + (ATTN_WIDTH,))
        zi = jnp.zeros(lead + (LANES - MLSTM_HEADS,), w.dtype)
        zf = jnp.zeros(lead + (LANES - 2 * MLSTM_HEADS,), w.dtype)
        return jnp.concatenate([aq, ak, av, mq, mk, mv, mo, cx, cb, cc, gates,
                                mi, zi, mf, mf, zf], axis=-1)
    return cols(w_in).astype(BF16), cols(b_in[None, :])


def _mixer_tile(seq):
    return 256 if seq % 256 == 0 else ATTN_BLOCK


def _mixer_layer(x, positions, w_in, b_in, sinks, norm_w, conv_w, w_branch, w_out, ln_w, ln_b):
    bsz, seq, d = x.shape
    tm = _mixer_tile(seq)
    ns = seq // tm
    nblk = tm // ATTN_BLOCK
    win, bin_ = _pack_in_proj(w_in, b_in)
    wb0 = w_branch[0].reshape(N_Q_HEADS, HEAD_DIM, d)[Q_HEAD_ORDER, :, :].reshape(BRANCH_WIDTH, d)
    wb = jnp.concatenate([wb0[None], w_branch[1:]], axis=0).astype(BF16)
    wo = w_out.astype(BF16)
    pos = positions.reshape(bsz * ns, nblk, ATTN_BLOCK)

    def const(shape):
        return pl.BlockSpec(shape, lambda b, s, *_: (0,) * len(shape),
                            pipeline_mode=pl.Buffered(1))

    grid_spec = pltpu.PrefetchScalarGridSpec(
        num_scalar_prefetch=1,
        grid=(bsz, ns),
        in_specs=[
            pl.BlockSpec((1, tm, d), lambda b, s, *_: (b, s, 0)),
            pl.BlockSpec((1, nblk, ATTN_BLOCK), lambda b, s, *_: (b * ns + s, 0, 0)),
            const((SUBLANES, LANES)),
            const((d, P_PACKED)),
            const((1, P_PACKED)),
            const((1, MLSTM_WIDTH)),
            const((CONV_K, CONV_WIDTH)),
            const((N_BRANCH, BRANCH_WIDTH, d)),
            const((d, d)),
            const((1, d)),
            const((1, d)),
        ],
        out_specs=pl.BlockSpec((1, tm, d), lambda b, s, *_: (b, s, 0)),
        scratch_shapes=[
            pltpu.VMEM((ATTN_BLOCK, LANES), BF16),
            pltpu.VMEM((ATTN_BLOCK, LANES), BF16),
            pltpu.VMEM((MLSTM_HEADS, MLSTM_HEAD_DIM, 2 * MLSTM_HEAD_DIM), F32),
            pltpu.VMEM((MLSTM_HEADS, SUBLANES, LANES), F32),
            pltpu.VMEM((tm + 2 * SUBLANES, CONV_WIDTH), F32),
            pltpu.VMEM((tm, N_BRANCH * BRANCH_WIDTH), BF16),
        ],
    )
    return pl.pallas_call(
        functools.partial(_mixer_kernel, tm=tm),
        out_shape=jax.ShapeDtypeStruct((bsz, seq, d), F32),
        grid_spec=grid_spec,
        compiler_params=pltpu.CompilerParams(
            dimension_semantics=("arbitrary", "arbitrary"),
            vmem_limit_bytes=VMEM_LIMIT_BYTES),
        name="token_mixer",
    )(sinks, x, pos, _rope_table(), win, bin_, norm_w.reshape(1, MLSTM_WIDTH), conv_w, wb, wo,
      ln_w.reshape(1, d), ln_b.reshape(1, d))


def kernel(x, positions, w_in, b_in, attn_sinks, mlstm_norm_w, conv_w, w_branch, w_out,
           ffn1_w_gu, ffn1_w_down, ffn2_w_gu, ffn2_w_down, ln_w, ln_b):
    bsz, seq, d = x.shape
    for l in range(DEPTH):
        x = _ffn_layer(x.reshape(bsz * seq, d), ffn1_w_gu[l], ffn1_w_down[l],
                       ln_w[l, 0], ln_b[l, 0]).reshape(bsz, seq, d)
        x = _mixer_layer(x, positions, w_in[l], b_in[l], attn_sinks[l], mlstm_norm_w[l],
                         conv_w[l], w_branch[l], w_out[l], ln_w[l, 1], ln_b[l, 1])
        x = _ffn_layer(x.reshape(bsz * seq, d), ffn2_w_gu[l], ffn2_w_down[l],
                       ln_w[l, 2], ln_b[l, 2]).reshape(bsz, seq, d)
    return x
```

```python
import functools
import math

import numpy as np
import jax
import jax.numpy as jnp
from jax import lax
from jax.experimental import pallas as pl
from jax.experimental.pallas import tpu as pltpu

D_MODEL = 1024
DEPTH = 2

HEAD_DIM = 64
N_Q_HEADS = 8
N_KV_HEADS = 2
GROUP = N_Q_HEADS // N_KV_HEADS
WINDOW = 128
ATTN_BLOCK = 128
ROPE_THETA = 500000.0
ROT_DIM = HEAD_DIM // 4
ATTN_WIDTH = N_Q_HEADS * HEAD_DIM
KV_WIDTH = N_KV_HEADS * HEAD_DIM

MLSTM_HEADS = 4
MLSTM_HEAD_DIM = D_MODEL // 8
MLSTM_WIDTH = MLSTM_HEADS * MLSTM_HEAD_DIM
MLSTM_CHUNK = 128

CONV_WIDTH = D_MODEL // 2
CONV_K = 3

N_BRANCH = 3
BRANCH_WIDTH = 512
D_FF = 2816
LN_EPS = 1e-5
ALPHA = (2.0 * DEPTH) ** 0.25
NEG_INF = -1e30

SIZES = (ATTN_WIDTH, KV_WIDTH, KV_WIDTH,
         MLSTM_WIDTH, MLSTM_WIDTH, MLSTM_WIDTH, MLSTM_WIDTH, MLSTM_HEADS, MLSTM_HEADS,
         CONV_WIDTH, CONV_WIDTH, CONV_WIDTH,
         N_BRANCH * D_MODEL)

LANES = 128
SUBLANES = 8
VMEM_LIMIT_BYTES = 56 * 1024 * 1024

OFF_AQ = 0
OFF_AK = OFF_AQ + ATTN_WIDTH
OFF_AV = OFF_AK + KV_WIDTH
OFF_M = OFF_AV + KV_WIDTH
OFF_C = OFF_M + 4 * MLSTM_WIDTH
OFF_G = OFF_C + 3 * CONV_WIDTH
OFF_IF = OFF_G + N_BRANCH * D_MODEL
P_PACKED = OFF_IF + 2 * LANES

Q_HEAD_ORDER = (0, 4, 1, 5, 2, 6, 3, 7)

BF16 = jnp.bfloat16
F32 = jnp.float32


def _dot(a, b):
    return jnp.dot(a, b, preferred_element_type=F32)


def _dot_nt(a, b):
    return lax.dot_general(a, b, (((1,), (1,)), ((), ())), preferred_element_type=F32)


def _dot_tn(a, b):
    return lax.dot_general(a, b, (((0,), (0,)), ((), ())), preferred_element_type=F32)


def _layer_norm_rows(y, w, b):
    mu = jnp.mean(y, axis=-1, keepdims=True)
    yc = y - mu
    var = jnp.mean(yc * yc, axis=-1, keepdims=True)
    return yc * lax.rsqrt(var + LN_EPS) * w + b


def _sigmoid(x):
    return 1.0 / (1.0 + jnp.exp(-x))


def _log_sigmoid(x):
    return jnp.minimum(x, 0.0) - jnp.log1p(jnp.exp(-jnp.abs(x)))


def _ffn_kernel(x_ref, wgu_ref, wd_ref, lnw_ref, lnb_ref, o_ref, *, chunks):
    d_ff = wd_ref.shape[0]
    x = x_ref[...]
    xb = x.astype(BF16)
    acc = None
    for c0, c1 in chunks:
        g = _dot(xb, wgu_ref[:, c0:c1])
        u = _dot(xb, wgu_ref[:, d_ff + c0:d_ff + c1])
        a = (g * _sigmoid(g) * u).astype(BF16)
        part = _dot(a, wd_ref[c0:c1, :])
        acc = part if acc is None else acc + part
    y = ALPHA * x + 0.5 * acc
    o_ref[...] = _layer_norm_rows(y, lnw_ref[...], lnb_ref[...])


def _ffn_tiles(n_tokens, d_ff):
    tm = 512 if n_tokens % 512 == 0 else 128
    step = 2 * LANES
    chunks = tuple((c, min(c + step, d_ff)) for c in range(0, d_ff, step))
    return tm, chunks


def _ffn_layer(x2d, w_gu, w_down, ln_w, ln_b):
    n, d = x2d.shape
    d_ff = w_down.shape[0]
    tm, chunks = _ffn_tiles(n, d_ff)

    def const(shape):
        return pl.BlockSpec(shape, lambda i: (0,) * len(shape), pipeline_mode=pl.Buffered(1))

    return pl.pallas_call(
        functools.partial(_ffn_kernel, chunks=chunks),
        out_shape=jax.ShapeDtypeStruct((n, d), F32),
        grid=(n // tm,),
        in_specs=[
            pl.BlockSpec((tm, d), lambda i: (i, 0)),
            const((d, 2 * d_ff)),
            const((d_ff, d)),
            const((1, d)),
            const((1, d)),
        ],
        out_specs=pl.BlockSpec((tm, d), lambda i: (i, 0)),
        compiler_params=pltpu.CompilerParams(
            dimension_semantics=("arbitrary",),
            vmem_limit_bytes=VMEM_LIMIT_BYTES),
        name="swiglu_ln",
    )(x2d, w_gu.astype(BF16), w_down.astype(BF16), ln_w.reshape(1, d), ln_b.reshape(1, d))


def _rope_tables(pos_row, tab):
    pos_t = jnp.broadcast_to(pos_row.astype(F32), (ATTN_BLOCK, LANES)).T
    ang = pos_t * tab[0:1, :]
    cos_f = jnp.cos(ang)
    sin_f = jnp.sin(ang)
    return cos_f, sin_f * tab[1:2, :], sin_f * tab[2:3, :]


def _rope(t, cos_f, sin_a, sin_b):
    half = ROT_DIM // 2
    up = pltpu.roll(t, LANES - half, axis=1)
    down = pltpu.roll(t, half, axis=1)
    return t * cos_f + up * sin_a + down * sin_b


def _attention_block(q_blk, k_cur, v_cur, k_prev, v_prev, thr, sinks_ref):
    L = ATTN_BLOCK
    lane = lax.broadcasted_iota(jnp.int32, (1, LANES), 1)
    half0 = lane < HEAD_DIM
    zero = jnp.zeros((), BF16)

    def split(prev, cur):
        both = jnp.concatenate([prev, cur], axis=0)
        return jnp.concatenate([jnp.where(half0, both, zero),
                                jnp.where(half0, zero, both)], axis=0)

    kk = split(k_prev, k_cur)
    vv = split(v_prev, v_cur)

    qi = lax.broadcasted_iota(jnp.int32, (L, 4 * L), 0)
    kj = lax.broadcasted_iota(jnp.int32, (L, 4 * L), 1) & (2 * L - 1)
    rel = qi + L - kj
    mask = (rel >= 0) & (rel < WINDOW) & (kj >= thr)

    outs = []
    for c in range(N_Q_HEADS // 2):
        qc = q_blk[:, c * LANES:(c + 1) * LANES].astype(BF16)
        s = _dot_nt(qc, kk)
        s = jnp.where(mask, s, NEG_INF)
        ps, inv = [], []
        for g in range(N_KV_HEADS):
            sink = sinks_ref[Q_HEAD_ORDER[2 * c + g]]
            sg = s[:, g * 2 * L:(g + 1) * 2 * L]
            m = jnp.maximum(jnp.max(sg, axis=-1, keepdims=True), sink)
            p = jnp.exp(sg - m)
            den = jnp.sum(p, axis=-1, keepdims=True) + jnp.exp(sink - m)
            ps.append(p.astype(BF16))
            inv.append(1.0 / den)
        o = _dot(jnp.concatenate(ps, axis=1), vv)
        outs.append(o * jnp.where(half0, inv[0], inv[1]))
    return outs


def _mlstm_chunk(mproj, gif, s_ref, m_ref, nw_ref, row0):
    L = MLSTM_CHUNK
    dh = MLSTM_HEAD_DIM
    rows = slice(row0, row0 + L)
    lane = lax.broadcasted_iota(jnp.int32, (1, LANES), 1)
    gi = gif[rows, 0:LANES]
    gf = gif[rows, LANES:2 * LANES]
    lf = jnp.where(lane < 2 * MLSTM_HEADS, _log_sigmoid(gf), 0.0)

    r_i = lax.broadcasted_iota(jnp.int32, (L, L), 0)
    c_i = lax.broadcasted_iota(jnp.int32, (L, L), 1)
    causal = r_i >= c_i
    tril = jnp.where(causal, 1.0, 0.0).astype(BF16)
    hi = lf.astype(BF16)
    r1 = lf - hi.astype(F32)
    mid = r1.astype(BF16)
    lo = (r1 - mid.astype(F32)).astype(BF16)
    bsum = _dot(tril, hi) + _dot(tril, mid) + _dot(tril, lo)
    t_col = jnp.where(lane < MLSTM_HEADS, gi - bsum, bsum)
    t_row = t_col.T

    ones_col = jnp.where(lane == 0, 1.0, 0.0).astype(BF16)
    ones_tile = jnp.broadcast_to(ones_col, (L, LANES))
    scale = dh ** -0.5

    outs = []
    for h in range(MLSTM_HEADS):
        q = mproj[rows, h * dh:(h + 1) * dh].astype(BF16)
        k = mproj[rows, MLSTM_WIDTH + h * dh:MLSTM_WIDTH + (h + 1) * dh] * scale
        v = mproj[rows, 2 * MLSTM_WIDTH + h * dh:2 * MLSTM_WIDTH + (h + 1) * dh].astype(BF16)
        og = mproj[rows, 3 * MLSTM_WIDTH + h * dh:3 * MLSTM_WIDTH + (h + 1) * dh]
        b_col = t_col[:, MLSTM_HEADS + h:MLSTM_HEADS + h + 1]
        d_col = t_col[:, h:h + 1]
        d_row = t_row[h:h + 1, :]
        m_prev = m_ref[h][0:1, 0:1]
        state = s_ref[h]

        logw = jnp.where(causal, b_col + d_row, NEG_INF)
        inter = b_col + m_prev
        m_t = jnp.maximum(inter, jnp.max(logw, axis=-1, keepdims=True))
        w_inter = jnp.exp(inter - m_t)
        scores = _dot_nt(q, k.astype(BF16)) * jnp.exp(logw - m_t)
        row_sum = jnp.sum(scores, axis=-1, keepdims=True)
        from_state = _dot(q, state.astype(BF16))
        num = w_inter * from_state[:, 0:dh] + _dot(scores.astype(BF16), v)
        den = w_inter * from_state[:, dh:dh + 1] + row_sum
        hh = num / jnp.maximum(jnp.abs(den), jnp.exp(-m_t))

        m_new = m_t[L - 1:L, :]
        b_last = b_col[L - 1:L, :]
        decay = jnp.exp(b_last + m_prev - m_new)
        ws = jnp.exp(b_last + d_col - m_new)
        kw = (k * ws).astype(BF16)
        v_ext = jnp.concatenate([v, ones_tile], axis=1)
        s_ref[h] = decay * state + _dot_tn(kw, v_ext)
        m_ref[h] = jnp.broadcast_to(m_new, (SUBLANES, LANES))

        mu = jnp.mean(hh, axis=-1, keepdims=True)
        hc = hh - mu
        hn = hc * lax.rsqrt(jnp.mean(hc * hc, axis=-1, keepdims=True) + LN_EPS)
        outs.append(hn * nw_ref[:, h * dh:(h + 1) * dh] * _sigmoid(og))
    return outs


def _mixer_kernel(sinks_ref, x_ref, pos_ref, tab_ref, win_ref, bin_ref, nw_ref, cw_ref,
                  wb_ref, wo_ref, lnw_ref, lnb_ref, o_ref,
                  kprev_ref, vprev_ref, s_ref, m_ref, conv_ref, y_ref, *, tm):
    s_idx = pl.program_id(1)
    L = ATTN_BLOCK
    nblk = tm // L

    @pl.when(s_idx == 0)
    def _reset():
        kprev_ref[...] = jnp.zeros_like(kprev_ref)
        vprev_ref[...] = jnp.zeros_like(vprev_ref)
        s_ref[...] = jnp.zeros_like(s_ref)
        m_ref[...] = jnp.zeros_like(m_ref)
        conv_ref[0:SUBLANES, :] = jnp.zeros((SUBLANES, CONV_WIDTH), F32)

    x = x_ref[0]
    xb = x.astype(BF16)

    def proj(lo, hi):
        return _dot(xb, win_ref[:, lo:hi]) + bin_ref[:, lo:hi]

    qkv = proj(OFF_AQ, OFF_M)
    tab = tab_ref[...]
    for j in range(nblk):
        rows = slice(j * L, (j + 1) * L)
        cos_f, sin_a, sin_b = _rope_tables(pos_ref[0, j:j + 1, :], tab)
        q_blk = jnp.concatenate(
            [_rope(qkv[rows, c * LANES:(c + 1) * LANES], cos_f, sin_a, sin_b) * (HEAD_DIM ** -0.5)
             for c in range(N_Q_HEADS // 2)], axis=1)
        k_cur = _rope(qkv[rows, OFF_AK:OFF_AK + LANES], cos_f, sin_a, sin_b).astype(BF16)
        v_cur = qkv[rows, OFF_AV:OFF_AV + LANES].astype(BF16)
        if j == 0:
            k_prev, v_prev = kprev_ref[...], vprev_ref[...]
            thr = jnp.where(s_idx == 0, L, 0)
        else:
            thr = 0
        outs = _attention_block(q_blk, k_cur, v_cur, k_prev, v_prev, thr, sinks_ref)
        for c, o in enumerate(outs):
            y_ref[rows, c * LANES:(c + 1) * LANES] = o.astype(BF16)
        k_prev, v_prev = k_cur, v_cur
    kprev_ref[...] = k_prev
    vprev_ref[...] = v_prev

    mproj = proj(OFF_M, OFF_C)
    gif = proj(OFF_IF, P_PACKED)
    for j in range(nblk):
        outs = _mlstm_chunk(mproj, gif, s_ref, m_ref, nw_ref, j * L)
        for h, o in enumerate(outs):
            y_ref[j * L:(j + 1) * L,
                  BRANCH_WIDTH + h * MLSTM_HEAD_DIM:BRANCH_WIDTH + (h + 1) * MLSTM_HEAD_DIM] = o.astype(BF16)

    cproj = proj(OFF_C, OFF_G)
    u = cproj[:, 2 * CONV_WIDTH:] * cproj[:, 0:CONV_WIDTH]
    conv_ref[SUBLANES:SUBLANES + tm, :] = u
    u1 = conv_ref[pl.ds(SUBLANES - 1, tm), :]
    u2 = conv_ref[pl.ds(SUBLANES - 2, tm), :]
    yc = cw_ref[0:1, :] * u2 + cw_ref[1:2, :] * u1 + cw_ref[2:3, :] * u
    y_ref[:, 2 * BRANCH_WIDTH:] = (cproj[:, CONV_WIDTH:2 * CONV_WIDTH] * yc).astype(BF16)
    conv_ref[0:SUBLANES, :] = conv_ref[tm:tm + SUBLANES, :]

    merged = None
    for b in range(N_BRANCH):
        gate = _sigmoid(proj(OFF_G + b * D_MODEL, OFF_G + (b + 1) * D_MODEL))
        term = gate * _dot(y_ref[:, b * BRANCH_WIDTH:(b + 1) * BRANCH_WIDTH], wb_ref[b])
        merged = term if merged is None else merged + term
    mix = _dot(merged.astype(BF16), wo_ref[...])
    y = ALPHA * x + mix
    o_ref[0] = _layer_norm_rows(y, lnw_ref[...], lnb_ref[...])


def _rope_table():
    lane = np.arange(LANES)
    l64 = lane % HEAD_DIM
    half = ROT_DIM // 2
    inv_freq = ROPE_THETA ** (-np.arange(0, ROT_DIM, 2, dtype=np.float64) / ROT_DIM)
    tab = np.zeros((SUBLANES, LANES), np.float32)
    tab[0] = np.where(l64 < ROT_DIM, inv_freq[l64 % half], 0.0)
    tab[1] = np.where(l64 < half, -1.0, 0.0)
    tab[2] = np.where((l64 >= half) & (l64 < ROT_DIM), 1.0, 0.0)
    return jnp.asarray(tab)


def _pack_in_proj(w_in, b_in):
    def cols(w):
        pts = np.cumsum(SIZES)[:-1].tolist()
        (aq, ak, av, mq, mk, mv, mo, mi, mf, cx, cb, cc, gates) = jnp.split(w, pts, axis=-1)
        lead = w.shape[:-1]
        aq = aq.reshape(lead + (N_Q_HEADS, HEAD_DIM))[..., Q_HEAD_ORDER, :].reshape(lead + (ATTN_WIDTH,))
        zi = jnp.zeros(lead + (LANES - MLSTM_HEADS,), w.dtype)
        zf = jnp.zeros(lead + (LANES - 2 * MLSTM_HEADS,), w.dtype)
        return jnp.concatenate([aq, ak, av, mq, mk, mv, mo, cx, cb, cc, gates,
                                mi, zi, mf, mf, zf], axis=-1)
    return cols(w_in).astype(BF16), cols(b_in[None, :])


def _mixer_tile(seq):
    return 256 if seq % 256 == 0 else ATTN_BLOCK


def _mixer_layer(x, positions, w_in, b_in, sinks, norm_w, conv_w, w_branch, w_out, ln_w, ln_b):
    bsz, seq, d = x.shape
    tm = _mixer_tile(seq)
    ns = seq // tm
    nblk = tm // ATTN_BLOCK
    win, bin_ = _pack_in_proj(w_in, b_in)
    wb0 = w_branch[0].reshape(N_Q_HEADS, HEAD_DIM, d)[Q_HEAD_ORDER, :, :].reshape(BRANCH_WIDTH, d)
    wb = jnp.concatenate([wb0[None], w_branch[1:]], axis=0).astype(BF16)
    wo = w_out.astype(BF16)
    pos = positions.reshape(bsz * ns, nblk, ATTN_BLOCK)

    def const(shape):
        return pl.BlockSpec(shape, lambda b, s, *_: (0,) * len(shape),
                            pipeline_mode=pl.Buffered(1))

    grid_spec = pltpu.PrefetchScalarGridSpec(
        num_scalar_prefetch=1,
        grid=(bsz, ns),
        in_specs=[
            pl.BlockSpec((1, tm, d), lambda b, s, *_: (b, s, 0)),
            pl.BlockSpec((1, nblk, ATTN_BLOCK), lambda b, s, *_: (b * ns + s, 0, 0)),
            const((SUBLANES, LANES)),
            const((d, P_PACKED)),
            const((1, P_PACKED)),
            const((1, MLSTM_WIDTH)),
            const((CONV_K, CONV_WIDTH)),
            const((N_BRANCH, BRANCH_WIDTH, d)),
            const((d, d)),
            const((1, d)),
            const((1, d)),
        ],
        out_specs=pl.BlockSpec((1, tm, d), lambda b, s, *_: (b, s, 0)),
        scratch_shapes=[
            pltpu.VMEM((ATTN_BLOCK, LANES), BF16),
            pltpu.VMEM((ATTN_BLOCK, LANES), BF16),
            pltpu.VMEM((MLSTM_HEADS, MLSTM_HEAD_DIM, 2 * MLSTM_HEAD_DIM), F32),
            pltpu.VMEM((MLSTM_HEADS, SUBLANES, LANES), F32),
            pltpu.VMEM((tm + 2 * SUBLANES, CONV_WIDTH), F32),
            pltpu.VMEM((tm, N_BRANCH * BRANCH_WIDTH), BF16),
        ],
    )
    return pl.pallas_call(
        functools.partial(_mixer_kernel, tm=tm),
        out_shape=jax.ShapeDtypeStruct((bsz, seq, d), F32),
        grid_spec=grid_spec,
        compiler_params=pltpu.CompilerParams(
            dimension_semantics=("arbitrary", "arbitrary"),
            vmem_limit_bytes=VMEM_LIMIT_BYTES),
        name="token_mixer",
    )(sinks, x, pos, _rope_table(), win, bin_, norm_w.reshape(1, MLSTM_WIDTH), conv_w, wb, wo,
      ln_w.reshape(1, d), ln_b.reshape(1, d))


def kernel(x, positions, w_in, b_in, attn_sinks, mlstm_norm_w, conv_w, w_branch, w_out,
           ffn1_w_gu, ffn1_w_down, ffn2_w_gu, ffn2_w_down, ln_w, ln_b):
    bsz, seq, d = x.shape
    for l in range(DEPTH):
        x = _ffn_layer(x.reshape(bsz * seq, d), ffn1_w_gu[l], ffn1_w_down[l],
                       ln_w[l, 0], ln_b[l, 0]).reshape(bsz, seq, d)
        x = _mixer_layer(x, positions, w_in[l], b_in[l], attn_sinks[l], mlstm_norm_w[l],
                         conv_w[l], w_branch[l], w_out[l], ln_w[l, 1], ln_b[l, 1])
        x = _ffn_layer(x.reshape(bsz * seq, d), ffn2_w_gu[l], ffn2_w_down[l],
                       ln_w[l, 2], ln_b[l, 2]).reshape(bsz, seq, d)
    return x
```

```python
import functools
import math

import numpy as np
import jax
import jax.numpy as jnp
from jax import lax
from jax.experimental import pallas as pl
from jax.experimental.pallas import tpu as pltpu

D_MODEL = 1024
DEPTH = 2

HEAD_DIM = 64
N_Q_HEADS = 8
N_KV_HEADS = 2
GROUP = N_Q_HEADS // N_KV_HEADS
WINDOW = 128
ATTN_BLOCK = 128
ROPE_THETA = 500000.0
ROT_DIM = HEAD_DIM // 4
ATTN_WIDTH = N_Q_HEADS * HEAD_DIM
KV_WIDTH = N_KV_HEADS * HEAD_DIM

MLSTM_HEADS = 4
MLSTM_HEAD_DIM = D_MODEL // 8
MLSTM_WIDTH = MLSTM_HEADS * MLSTM_HEAD_DIM
MLSTM_CHUNK = 128

CONV_WIDTH = D_MODEL // 2
CONV_K = 3

N_BRANCH = 3
BRANCH_WIDTH = 512
D_FF = 2816
LN_EPS = 1e-5
ALPHA = (2.0 * DEPTH) ** 0.25
NEG_INF = -1e30

SIZES = (ATTN_WIDTH, KV_WIDTH, KV_WIDTH,
         MLSTM_WIDTH, MLSTM_WIDTH, MLSTM_WIDTH, MLSTM_WIDTH, MLSTM_HEADS, MLSTM_HEADS,
         CONV_WIDTH, CONV_WIDTH, CONV_WIDTH,
         N_BRANCH * D_MODEL)

LANES = 128
SUBLANES = 8
VMEM_LIMIT_BYTES = 56 * 1024 * 1024

OFF_AQ = 0
OFF_AK = OFF_AQ + ATTN_WIDTH
OFF_AV = OFF_AK + KV_WIDTH
OFF_M = OFF_AV + KV_WIDTH
OFF_C = OFF_M + 4 * MLSTM_WIDTH
OFF_G = OFF_C + 3 * CONV_WIDTH
OFF_IF = OFF_G + N_BRANCH * D_MODEL
P_PACKED = OFF_IF + 2 * LANES

Q_HEAD_ORDER = (0, 4, 1, 5, 2, 6, 3, 7)

BF16 = jnp.bfloat16
F32 = jnp.float32


def _dot(a, b):
    return jnp.dot(a, b, preferred_element_type=F32)


def _dot_nt(a, b):
    return lax.dot_general(a, b, (((1,), (1,)), ((), ())), preferred_element_type=F32)


def _dot_tn(a, b):
    return lax.dot_general(a, b, (((0,), (0,)), ((), ())), preferred_element_type=F32)


def _layer_norm_rows(y, w, b):
    mu = jnp.mean(y, axis=-1, keepdims=True)
    yc = y - mu
    var = jnp.mean(yc * yc, axis=-1, keepdims=True)
    return yc * lax.rsqrt(var + LN_EPS) * w + b


def _sigmoid(x):
    return 0.5 * jnp.tanh(0.5 * x) + 0.5


def _log_sigmoid(x):
    return jnp.minimum(x, 0.0) - jnp.log1p(jnp.exp(-jnp.abs(x)))


def _ffn_kernel(x_ref, wgu_ref, wd_ref, lnw_ref, lnb_ref, o_ref, *, chunks):
    d_ff = wd_ref.shape[0]
    x = x_ref[...]
    xb = x.astype(BF16)
    acc = None
    for c0, c1 in chunks:
        g = _dot(xb, wgu_ref[:, c0:c1])
        u = _dot(xb, wgu_ref[:, d_ff + c0:d_ff + c1])
        a = (g * _sigmoid(g) * u).astype(BF16)
        part = _dot(a, wd_ref[c0:c1, :])
        acc = part if acc is None else acc + part
    y = ALPHA * x + 0.5 * acc
    o_ref[...] = _layer_norm_rows(y, lnw_ref[...], lnb_ref[...])


def _ffn_tiles(n_tokens, d_ff):
    tm = 512 if n_tokens % 512 == 0 else 128
    step = 2 * LANES
    chunks = tuple((c, min(c + step, d_ff)) for c in range(0, d_ff, step))
    return tm, chunks


def _ffn_layer(x2d, w_gu, w_down, ln_w, ln_b):
    n, d = x2d.shape
    d_ff = w_down.shape[0]
    tm, chunks = _ffn_tiles(n, d_ff)

    def const(shape):
        return pl.BlockSpec(shape, lambda i: (0,) * len(shape), pipeline_mode=pl.Buffered(1))

    return pl.pallas_call(
        functools.partial(_ffn_kernel, chunks=chunks),
        out_shape=jax.ShapeDtypeStruct((n, d), F32),
        grid=(n // tm,),
        in_specs=[
            pl.BlockSpec((tm, d), lambda i: (i, 0)),
            const((d, 2 * d_ff)),
            const((d_ff, d)),
            const((1, d)),
            const((1, d)),
        ],
        out_specs=pl.BlockSpec((tm, d), lambda i: (i, 0)),
        compiler_params=pltpu.CompilerParams(
            dimension_semantics=("arbitrary",),
            vmem_limit_bytes=VMEM_LIMIT_BYTES),
        name="swiglu_ln",
    )(x2d, w_gu.astype(BF16), w_down.astype(BF16), ln_w.reshape(1, d), ln_b.reshape(1, d))


def _rope_tables(pos_row, tab):
    pos_t = jnp.broadcast_to(pos_row.astype(F32), (ATTN_BLOCK, LANES)).T
    ang = pos_t * tab[0:1, :]
    cos_f = jnp.cos(ang)
    sin_f = jnp.sin(ang)
    return cos_f, sin_f * tab[1:2, :], sin_f * tab[2:3, :]


def _rope(t, cos_f, sin_a, sin_b):
    half = ROT_DIM // 2
    up = pltpu.roll(t, LANES - half, axis=1)
    down = pltpu.roll(t, half, axis=1)
    return t * cos_f + up * sin_a + down * sin_b


def _split_kv(prev, cur):
    lane = lax.broadcasted_iota(jnp.int32, (1, LANES), 1)
    half0 = lane < HEAD_DIM
    zero = jnp.zeros((), BF16)
    both = jnp.concatenate([prev, cur], axis=0)
    return jnp.concatenate([jnp.where(half0, both, zero),
                            jnp.where(half0, zero, both)], axis=0)


def _attention_scores(q_all, k_cur, k_prev):
    return _dot_nt(q_all, _split_kv(k_prev, k_cur))


def _attention_softmax(s_all, thr, sinks_ref):
    L = ATTN_BLOCK
    lane = lax.broadcasted_iota(jnp.int32, (1, LANES), 1)
    half0 = lane < HEAD_DIM
    qi = lax.broadcasted_iota(jnp.int32, (L, 2 * L), 0)
    kj = lax.broadcasted_iota(jnp.int32, (L, 2 * L), 1)
    rel = qi + L - kj
    mask = (rel >= 0) & (rel < WINDOW) & (kj >= thr)
    ps, scales = [], []
    for c in range(N_Q_HEADS // 2):
        pg, inv = [], []
        for g in range(N_KV_HEADS):
            sink = sinks_ref[Q_HEAD_ORDER[2 * c + g]]
            sg = jnp.where(mask, s_all[c * L:(c + 1) * L, g * 2 * L:(g + 1) * 2 * L], NEG_INF)
            m = jnp.maximum(jnp.max(sg, axis=-1, keepdims=True), sink)
            p = jnp.exp(sg - m)
            den = jnp.sum(p, axis=-1, keepdims=True) + jnp.exp(sink - m)
            pg.append(p.astype(BF16))
            inv.append(1.0 / den)
        ps.append(jnp.concatenate(pg, axis=1))
        scales.append(jnp.where(half0, inv[0], inv[1]))
    return jnp.concatenate(ps, axis=0), jnp.concatenate(scales, axis=0)


def _mlstm_gates(gif, row0):
    L = MLSTM_CHUNK
    rows = slice(row0, row0 + L)
    lane = lax.broadcasted_iota(jnp.int32, (1, LANES), 1)
    gi = gif[rows, 0:LANES]
    gf = gif[rows, LANES:2 * LANES]
    lf = jnp.where(lane < 2 * MLSTM_HEADS, _log_sigmoid(gf), 0.0)
    r_i = lax.broadcasted_iota(jnp.int32, (L, L), 0)
    c_i = lax.broadcasted_iota(jnp.int32, (L, L), 1)
    tril = jnp.where(r_i >= c_i, 1.0, 0.0).astype(BF16)
    hi = lf.astype(BF16)
    r1 = lf - hi.astype(F32)
    mid = r1.astype(BF16)
    lo = (r1 - mid.astype(F32)).astype(BF16)
    bsum = _dot(tril, jnp.concatenate([hi, mid, lo], axis=1))
    bsum = bsum[:, 0:LANES] + bsum[:, LANES:2 * LANES] + bsum[:, 2 * LANES:]
    t_col = jnp.where(lane < MLSTM_HEADS, gi - bsum, bsum)
    return t_col, t_col.T


def _mlstm_stage1(mproj, t_col, t_row, s_ref, m_ref, row0):
    L = MLSTM_CHUNK
    dh = MLSTM_HEAD_DIM
    rows = slice(row0, row0 + L)
    r_i = lax.broadcasted_iota(jnp.int32, (L, L), 0)
    c_i = lax.broadcasted_iota(jnp.int32, (L, L), 1)
    causal = r_i >= c_i
    scale = dh ** -0.5
    heads = []
    for h in range(MLSTM_HEADS):
        q = mproj[rows, h * dh:(h + 1) * dh].astype(BF16)
        k = mproj[rows, MLSTM_WIDTH + h * dh:MLSTM_WIDTH + (h + 1) * dh] * scale
        state = s_ref[h]
        qk = _dot_nt(q, k.astype(BF16))
        from_state = _dot(q, state.astype(BF16))
        heads.append(dict(k=k, state=state, qk=qk, from_state=from_state))
    for h, hd in enumerate(heads):
        b_col = t_col[:, MLSTM_HEADS + h:MLSTM_HEADS + h + 1]
        d_col = t_col[:, h:h + 1]
        d_row = t_row[h:h + 1, :]
        m_prev = m_ref[h][0:1, 0:1]
        logw = jnp.where(causal, b_col + d_row, NEG_INF)
        inter = b_col + m_prev
        m_t = jnp.maximum(inter, jnp.max(logw, axis=-1, keepdims=True))
        scores = hd["qk"] * jnp.exp(logw - m_t)
        m_new = m_t[L - 1:L, :]
        b_last = b_col[L - 1:L, :]
        ws = jnp.exp(b_last + d_col - m_new)
        hd.update(m_t=m_t, w_inter=jnp.exp(inter - m_t), scores=scores,
                  row_sum=jnp.sum(scores, axis=-1, keepdims=True),
                  decay=jnp.exp(b_last + m_prev - m_new), m_new=m_new,
                  kw=(hd["k"] * ws).astype(BF16))
    return heads


def _mlstm_stage2(mproj, heads, s_ref, m_ref, nw_ref, row0):
    L = MLSTM_CHUNK
    dh = MLSTM_HEAD_DIM
    rows = slice(row0, row0 + L)
    lane = lax.broadcasted_iota(jnp.int32, (1, LANES), 1)
    ones_tile = jnp.broadcast_to(jnp.where(lane == 0, 1.0, 0.0).astype(BF16), (L, LANES))
    outs = []
    for h, hd in enumerate(heads):
        v = mproj[rows, 2 * MLSTM_WIDTH + h * dh:2 * MLSTM_WIDTH + (h + 1) * dh].astype(BF16)
        hd["intra"] = _dot(hd["scores"].astype(BF16), v)
        v_ext = jnp.concatenate([v, ones_tile], axis=1)
        s_ref[h] = hd["decay"] * hd["state"] + _dot_tn(hd["kw"], v_ext)
        m_ref[h] = jnp.broadcast_to(hd["m_new"], (SUBLANES, LANES))
    for h, hd in enumerate(heads):
        og = mproj[rows, 3 * MLSTM_WIDTH + h * dh:3 * MLSTM_WIDTH + (h + 1) * dh]
        num = hd["w_inter"] * hd["from_state"][:, 0:dh] + hd["intra"]
        den = hd["w_inter"] * hd["from_state"][:, dh:dh + 1] + hd["row_sum"]
        hh = num / jnp.maximum(jnp.abs(den), jnp.exp(-hd["m_t"]))
        mu = jnp.mean(hh, axis=-1, keepdims=True)
        hc = hh - mu
        hn = hc * lax.rsqrt(jnp.mean(hc * hc, axis=-1, keepdims=True) + LN_EPS)
        outs.append(hn * nw_ref[:, h * dh:(h + 1) * dh] * _sigmoid(og))
    return outs


def _mixer_kernel(sinks_ref, x_ref, pos_ref, tab_ref, win_ref, bin_ref, nw_ref, cw_ref,
                  wb_ref, wo_ref, lnw_ref, lnb_ref, o_ref,
                  kprev_ref, vprev_ref, s_ref, m_ref, conv_ref, y_ref, *, tm, tsub):
    s_idx = pl.program_id(1)

    @pl.when(s_idx == 0)
    def _reset():
        kprev_ref[...] = jnp.zeros_like(kprev_ref)
        vprev_ref[...] = jnp.zeros_like(vprev_ref)
        s_ref[...] = jnp.zeros_like(s_ref)
        m_ref[...] = jnp.zeros_like(m_ref)
        conv_ref[0:SUBLANES, :] = jnp.zeros((SUBLANES, CONV_WIDTH), F32)

    kv = (kprev_ref[...], vprev_ref[...])
    for sub in range(tm // tsub):
        kv = _mixer_subtile(sub, kv, s_idx, sinks_ref, x_ref, pos_ref, tab_ref, win_ref, bin_ref,
                            nw_ref, cw_ref, wb_ref, wo_ref, lnw_ref, lnb_ref, o_ref,
                            s_ref, m_ref, conv_ref, y_ref, tsub)
    kprev_ref[...], vprev_ref[...] = kv
    conv_ref[0:SUBLANES, :] = conv_ref[tm:tm + SUBLANES, :]


def _mixer_subtile(sub, kv, s_idx, sinks_ref, x_ref, pos_ref, tab_ref, win_ref, bin_ref,
                   nw_ref, cw_ref, wb_ref, wo_ref, lnw_ref, lnb_ref, o_ref,
                   s_ref, m_ref, conv_ref, y_ref, tsub):
    L = ATTN_BLOCK
    nblk = tsub // L
    ngrp = N_Q_HEADS // 2
    r0 = sub * tsub
    x = x_ref[0, r0:r0 + tsub, :]
    xb = x.astype(BF16)

    def proj(lo, hi):
        return _dot(xb, win_ref[:, lo:hi]) + bin_ref[:, lo:hi]

    gates = []

    def fill():
        b = len(gates)
        if b < N_BRANCH:
            gates.append(_sigmoid(proj(OFF_G + b * D_MODEL, OFF_G + (b + 1) * D_MODEL)))

    tab = tab_ref[...]
    ropes = [_rope_tables(pos_ref[0, sub * nblk + j:sub * nblk + j + 1, :], tab) for j in range(nblk)]
    qkv = proj(OFF_AQ, OFF_M)
    gif = proj(OFF_IF, P_PACKED)
    mproj = proj(OFF_M, OFF_C)
    mgates = [_mlstm_gates(gif, j * L) for j in range(nblk)]
    cproj = proj(OFF_C, OFF_G)

    u = cproj[:, 2 * CONV_WIDTH:] * cproj[:, 0:CONV_WIDTH]
    conv_ref[SUBLANES + r0:SUBLANES + r0 + tsub, :] = u
    u1 = conv_ref[pl.ds(SUBLANES + r0 - 1, tsub), :]
    u2 = conv_ref[pl.ds(SUBLANES + r0 - 2, tsub), :]
    yc = cw_ref[0:1, :] * u2 + cw_ref[1:2, :] * u1 + cw_ref[2:3, :] * u
    y_ref[r0:r0 + tsub, 2 * BRANCH_WIDTH:] = (cproj[:, CONV_WIDTH:2 * CONV_WIDTH] * yc).astype(BF16)

    k_prev, v_prev = kv
    for j in range(nblk):
        rows = slice(j * L, (j + 1) * L)
        yrows = slice(r0 + j * L, r0 + (j + 1) * L)
        cos_f, sin_a, sin_b = ropes[j]
        q_all = jnp.concatenate(
            [(_rope(qkv[rows, c * LANES:(c + 1) * LANES], cos_f, sin_a, sin_b)
              * (HEAD_DIM ** -0.5)).astype(BF16) for c in range(ngrp)], axis=0)
        k_cur = _rope(qkv[rows, OFF_AK:OFF_AK + LANES], cos_f, sin_a, sin_b).astype(BF16)
        v_cur = qkv[rows, OFF_AV:OFF_AV + LANES].astype(BF16)
        thr = jnp.where(s_idx == 0, L, 0) if (sub == 0 and j == 0) else 0
        s_all = _attention_scores(q_all, k_cur, k_prev)
        heads = _mlstm_stage1(mproj, mgates[j][0], mgates[j][1], s_ref, m_ref, j * L)
        p_all, o_scale = _attention_softmax(s_all, thr, sinks_ref)
        fill()
        o_all = _dot(p_all, _split_kv(v_prev, v_cur)) * o_scale
        for c in range(ngrp):
            y_ref[yrows, c * LANES:(c + 1) * LANES] = o_all[c * L:(c + 1) * L, :].astype(BF16)
        outs = _mlstm_stage2(mproj, heads, s_ref, m_ref, nw_ref, j * L)
        for h, o in enumerate(outs):
            y_ref[yrows, BRANCH_WIDTH + h * MLSTM_HEAD_DIM:
                  BRANCH_WIDTH + (h + 1) * MLSTM_HEAD_DIM] = o.astype(BF16)
        k_prev, v_prev = k_cur, v_cur
        fill()
    while len(gates) < N_BRANCH:
        fill()

    merged = None
    for b in (2, 0, 1):
        term = gates[b] * _dot(y_ref[r0:r0 + tsub, b * BRANCH_WIDTH:(b + 1) * BRANCH_WIDTH], wb_ref[b])
        merged = term if merged is None else merged + term
    mix = _dot(merged.astype(BF16), wo_ref[...])
    y = ALPHA * x + mix
    o_ref[0, r0:r0 + tsub, :] = _layer_norm_rows(y, lnw_ref[...], lnb_ref[...])
    return k_prev, v_prev


def _rope_table():
    lane = np.arange(LANES)
    l64 = lane % HEAD_DIM
    half = ROT_DIM // 2
    inv_freq = ROPE_THETA ** (-np.arange(0, ROT_DIM, 2, dtype=np.float64) / ROT_DIM)
    tab = np.zeros((SUBLANES, LANES), np.float32)
    tab[0] = np.where(l64 < ROT_DIM, inv_freq[l64 % half], 0.0)
    tab[1] = np.where(l64 < half, -1.0, 0.0)
    tab[2] = np.where((l64 >= half) & (l64 < ROT_DIM), 1.0, 0.0)
    return jnp.asarray(tab)


def _pack_in_proj(w_in, b_in):
    def cols(w):
        pts = np.cumsum(SIZES)[:-1].tolist()
        (aq, ak, av, mq, mk, mv, mo, mi, mf, cx, cb, cc, gates) = jnp.split(w, pts, axis=-1)
        lead = w.shape[:-1]
        aq = aq.reshape(lead + (N_Q_HEADS, HEAD_DIM))[..., Q_HEAD_ORDER, :].reshape(lead + (ATTN_WIDTH,))
        zi = jnp.zeros(lead + (LANES - MLSTM_HEADS,), w.dtype)
        zf = jnp.zeros(lead + (LANES - 2 * MLSTM_HEADS,), w.dtype)
        return jnp.concatenate([aq, ak, av, mq, mk, mv, mo, cx, cb, cc, gates,
                                mi, zi, mf, mf, zf], axis=-1)
    return cols(w_in).astype(BF16), cols(b_in[None, :])


def _mixer_tile(seq):
    if seq % 512 == 0:
        return 512, 256
    if seq % 256 == 0:
        return 256, 256
    return ATTN_BLOCK, ATTN_BLOCK


def _mixer_layer(x, positions, w_in, b_in, sinks, norm_w, conv_w, w_branch, w_out, ln_w, ln_b):
    bsz, seq, d = x.shape
    tm, tsub = _mixer_tile(seq)
    ns = seq // tm
    nblk = tm // ATTN_BLOCK
    win, bin_ = _pack_in_proj(w_in, b_in)
    wb0 = w_branch[0].reshape(N_Q_HEADS, HEAD_DIM, d)[Q_HEAD_ORDER, :, :].reshape(BRANCH_WIDTH, d)
    wb = jnp.concatenate([wb0[None], w_branch[1:]], axis=0).astype(BF16)
    wo = w_out.astype(BF16)
    pos = positions.reshape(bsz * ns, nblk, ATTN_BLOCK)

    def const(shape):
        return pl.BlockSpec(shape, lambda b, s, *_: (0,) * len(shape),
                            pipeline_mode=pl.Buffered(1))

    grid_spec = pltpu.PrefetchScalarGridSpec(
        num_scalar_prefetch=1,
        grid=(bsz, ns),
        in_specs=[
            pl.BlockSpec((1, tm, d), lambda b, s, *_: (b, s, 0)),
            pl.BlockSpec((1, nblk, ATTN_BLOCK), lambda b, s, *_: (b * ns + s, 0, 0)),
            const((SUBLANES, LANES)),
            const((d, P_PACKED)),
            const((1, P_PACKED)),
            const((1, MLSTM_WIDTH)),
            const((CONV_K, CONV_WIDTH)),
            const((N_BRANCH, BRANCH_WIDTH, d)),
            const((d, d)),
            const((1, d)),
            const((1, d)),
        ],
        out_specs=pl.BlockSpec((1, tm, d), lambda b, s, *_: (b, s, 0)),
        scratch_shapes=[
            pltpu.VMEM((ATTN_BLOCK, LANES), BF16),
            pltpu.VMEM((ATTN_BLOCK, LANES), BF16),
            pltpu.VMEM((MLSTM_HEADS, MLSTM_HEAD_DIM, 2 * MLSTM_HEAD_DIM), F32),
            pltpu.VMEM((MLSTM_HEADS, SUBLANES, LANES), F32),
            pltpu.VMEM((tm + 2 * SUBLANES, CONV_WIDTH), F32),
            pltpu.VMEM((tm, N_BRANCH * BRANCH_WIDTH), BF16),
        ],
    )
    return pl.pallas_call(
        functools.partial(_mixer_kernel, tm=tm, tsub=tsub),
        out_shape=jax.ShapeDtypeStruct((bsz, seq, d), F32),
        grid_spec=grid_spec,
        compiler_params=pltpu.CompilerParams(
            dimension_semantics=("arbitrary", "arbitrary"),
            vmem_limit_bytes=VMEM_LIMIT_BYTES),
        name="token_mixer",
    )(sinks, x, pos, _rope_table(), win, bin_, norm_w.reshape(1, MLSTM_WIDTH), conv_w, wb, wo,
      ln_w.reshape(1, d), ln_b.reshape(1, d))


def kernel(x, positions, w_in, b_in, attn_sinks, mlstm_norm_w, conv_w, w_branch, w_out,
           ffn1_w_gu, ffn1_w_down, ffn2_w_gu, ffn2_w_down, ln_w, ln_b):
    bsz, seq, d = x.shape
    for l in range(DEPTH):
        x = _ffn_layer(x.reshape(bsz * seq, d), ffn1_w_gu[l], ffn1_w_down[l],
                       ln_w[l, 0], ln_b[l, 0]).reshape(bsz, seq, d)
        x = _mixer_layer(x, positions, w_in[l], b_in[l], attn_sinks[l], mlstm_norm_w[l],
                         conv_w[l], w_branch[l], w_out[l], ln_w[l, 1], ln_b[l, 1])
        x = _ffn_layer(x.reshape(bsz * seq, d), ffn2_w_gu[l], ffn2_w_down[l],
                       ln_w[l, 2], ln_b[l, 2]).reshape(bsz, seq, d)
    return x
```

```python
import functools
import math

import numpy as np
import jax
import jax.numpy as jnp
from jax import lax
from jax.experimental import pallas as pl
from jax.experimental.pallas import tpu as pltpu

D_MODEL = 1024
DEPTH = 2

HEAD_DIM = 64
N_Q_HEADS = 8
N_KV_HEADS = 2
GROUP = N_Q_HEADS // N_KV_HEADS
WINDOW = 128
ATTN_BLOCK = 128
ROPE_THETA = 500000.0
ROT_DIM = HEAD_DIM // 4
ATTN_WIDTH = N_Q_HEADS * HEAD_DIM
KV_WIDTH = N_KV_HEADS * HEAD_DIM

MLSTM_HEADS = 4
MLSTM_HEAD_DIM = D_MODEL // 8
MLSTM_WIDTH = MLSTM_HEADS * MLSTM_HEAD_DIM
MLSTM_CHUNK = 128

CONV_WIDTH = D_MODEL // 2
CONV_K = 3

N_BRANCH = 3
BRANCH_WIDTH = 512
D_FF = 2816
LN_EPS = 1e-5
ALPHA = (2.0 * DEPTH) ** 0.25
NEG_INF = -1e30

SIZES = (ATTN_WIDTH, KV_WIDTH, KV_WIDTH,
         MLSTM_WIDTH, MLSTM_WIDTH, MLSTM_WIDTH, MLSTM_WIDTH, MLSTM_HEADS, MLSTM_HEADS,
         CONV_WIDTH, CONV_WIDTH, CONV_WIDTH,
         N_BRANCH * D_MODEL)

LANES = 128
SUBLANES = 8
VMEM_LIMIT_BYTES = 56 * 1024 * 1024

OFF_AQ = 0
OFF_AK = OFF_AQ + ATTN_WIDTH
OFF_AV = OFF_AK + KV_WIDTH
OFF_M = OFF_AV + KV_WIDTH
OFF_C = OFF_M + 4 * MLSTM_WIDTH
OFF_G = OFF_C + 3 * CONV_WIDTH
OFF_IF = OFF_G + N_BRANCH * D_MODEL
P_PACKED = OFF_IF + 2 * LANES

Q_HEAD_ORDER = (0, 4, 1, 5, 2, 6, 3, 7)

BF16 = jnp.bfloat16
F32 = jnp.float32


def _dot(a, b):
    return jnp.dot(a, b, preferred_element_type=F32)


def _dot_nt(a, b):
    return lax.dot_general(a, b, (((1,), (1,)), ((), ())), preferred_element_type=F32)


def _dot_tn(a, b):
    return lax.dot_general(a, b, (((0,), (0,)), ((), ())), preferred_element_type=F32)


def _layer_norm_rows(y, w, b):
    mu = jnp.mean(y, axis=-1, keepdims=True)
    yc = y - mu
    var = jnp.mean(yc * yc, axis=-1, keepdims=True)
    return yc * lax.rsqrt(var + LN_EPS) * w + b


def _sigmoid(x):
    return 0.5 * jnp.tanh(0.5 * x) + 0.5


def _log_sigmoid(x):
    return jnp.minimum(x, 0.0) - jnp.log1p(jnp.exp(-jnp.abs(x)))


def _ffn_kernel(x_ref, wgu_ref, wd_ref, lnw_ref, lnb_ref, o_ref, *, chunks):
    d_ff = wd_ref.shape[0]
    x = x_ref[...]
    xb = x.astype(BF16)
    acc = None
    for c0, c1 in chunks:
        g = _dot(xb, wgu_ref[:, c0:c1])
        u = _dot(xb, wgu_ref[:, d_ff + c0:d_ff + c1])
        a = (g * _sigmoid(g) * u).astype(BF16)
        part = _dot(a, wd_ref[c0:c1, :])
        acc = part if acc is None else acc + part
    y = ALPHA * x + 0.5 * acc
    o_ref[...] = _layer_norm_rows(y, lnw_ref[...], lnb_ref[...])


def _ffn_tiles(n_tokens, d_ff):
    tm = 512 if n_tokens % 512 == 0 else 128
    step = 2 * LANES
    chunks = tuple((c, min(c + step, d_ff)) for c in range(0, d_ff, step))
    return tm, chunks


def _ffn_layer(x2d, layer, w_gu, w_down, ln_w, ln_b, ln_idx):
    n, d = x2d.shape
    d_ff = w_down.shape[1]
    tm, chunks = _ffn_tiles(n, d_ff)

    def const(shape, idx):
        return pl.BlockSpec(shape, lambda i: idx, pipeline_mode=pl.Buffered(1))

    return pl.pallas_call(
        functools.partial(_ffn_kernel, chunks=chunks),
        out_shape=jax.ShapeDtypeStruct((n, d), F32),
        grid=(n // tm,),
        in_specs=[
            pl.BlockSpec((tm, d), lambda i: (i, 0)),
            const((None, d, 2 * d_ff), (layer, 0, 0)),
            const((None, d_ff, d), (layer, 0, 0)),
            const((None, None, 1, d), (layer, ln_idx, 0, 0)),
            const((None, None, 1, d), (layer, ln_idx, 0, 0)),
        ],
        out_specs=pl.BlockSpec((tm, d), lambda i: (i, 0)),
        compiler_params=pltpu.CompilerParams(
            dimension_semantics=("arbitrary",),
            vmem_limit_bytes=VMEM_LIMIT_BYTES),
        name="swiglu_ln",
    )(x2d, w_gu, w_down, ln_w, ln_b)


def _rope_tables(pos_row, freq, expand):
    ang = freq * pos_row.astype(F32)
    pieces = []
    for t in (jnp.cos(ang), jnp.sin(ang)):
        hi = t.astype(BF16)
        r1 = t - hi.astype(F32)
        mid = r1.astype(BF16)
        lo = (r1 - mid.astype(F32)).astype(BF16)
        pieces += [hi, mid, lo]
    pieces += [jnp.ones((SUBLANES, LANES), BF16), jnp.zeros((SUBLANES, LANES), BF16)]
    cs = _dot_tn(jnp.concatenate(pieces, axis=0), expand)
    return cs[:, 0:LANES], cs[:, LANES:]


def _rope(t, cos_f, sin_f):
    half = ROT_DIM // 2
    lane = lax.broadcasted_iota(jnp.int32, (1, LANES), 1)
    up = pltpu.roll(t, LANES - half, axis=1)
    down = pltpu.roll(t, half, axis=1)
    return t * cos_f + jnp.where((lane & (HEAD_DIM - 1)) < half, up, down) * sin_f


def _split_kv(prev, cur):
    lane = lax.broadcasted_iota(jnp.int32, (1, LANES), 1)
    half0 = lane < HEAD_DIM
    zero = jnp.zeros((), BF16)
    both = jnp.concatenate([prev, cur], axis=0)
    return jnp.concatenate([jnp.where(half0, both, zero),
                            jnp.where(half0, zero, both)], axis=0)


def _attention_scores(q_all, k_cur, k_prev):
    return _dot_nt(q_all, _split_kv(k_prev, k_cur))


def _attention_softmax(s_all, thr, sinks_ref):
    L = ATTN_BLOCK
    lane = lax.broadcasted_iota(jnp.int32, (1, LANES), 1)
    half0 = lane < HEAD_DIM
    qi = lax.broadcasted_iota(jnp.int32, (L, 2 * L), 0)
    kj = lax.broadcasted_iota(jnp.int32, (L, 2 * L), 1)
    rel = qi + L - kj
    mask = (rel >= 0) & (rel < WINDOW) & (kj >= thr)
    ps, scales = [], []
    for c in range(N_Q_HEADS // 2):
        pg, inv = [], []
        for g in range(N_KV_HEADS):
            sink = sinks_ref[Q_HEAD_ORDER[2 * c + g]]
            sg = jnp.where(mask, s_all[c * L:(c + 1) * L, g * 2 * L:(g + 1) * 2 * L], NEG_INF)
            m = jnp.maximum(jnp.max(sg, axis=-1, keepdims=True), sink)
            p = jnp.exp(sg - m)
            den = jnp.sum(p, axis=-1, keepdims=True) + jnp.exp(sink - m)
            pg.append(p.astype(BF16))
            inv.append(1.0 / den)
        ps.append(jnp.concatenate(pg, axis=1))
        scales.append(jnp.where(half0, inv[0], inv[1]))
    return jnp.concatenate(ps, axis=0), jnp.concatenate(scales, axis=0)


def _mlstm_gates(gif, row0):
    L = MLSTM_CHUNK
    rows = slice(row0, row0 + L)
    lane = lax.broadcasted_iota(jnp.int32, (1, LANES), 1)
    gi = gif[rows, 0:LANES]
    gf = gif[rows, LANES:2 * LANES]
    lf = jnp.where(lane < 2 * MLSTM_HEADS, _log_sigmoid(gf), 0.0)
    r_i = lax.broadcasted_iota(jnp.int32, (L, L), 0)
    c_i = lax.broadcasted_iota(jnp.int32, (L, L), 1)
    tril = jnp.where(r_i >= c_i, 1.0, 0.0).astype(BF16)
    hi = lf.astype(BF16)
    r1 = lf - hi.astype(F32)
    mid = r1.astype(BF16)
    lo = (r1 - mid.astype(F32)).astype(BF16)
    bsum = _dot(tril, jnp.concatenate([hi, mid, lo], axis=1))
    bsum = bsum[:, 0:LANES] + bsum[:, LANES:2 * LANES] + bsum[:, 2 * LANES:]
    t_col = jnp.where(lane < MLSTM_HEADS, gi - bsum, bsum)
    return t_col, t_col.T


def _mlstm_stage1(mproj, t_col, t_row, s_ref, m_ref, row0):
    L = MLSTM_CHUNK
    dh = MLSTM_HEAD_DIM
    rows = slice(row0, row0 + L)
    r_i = lax.broadcasted_iota(jnp.int32, (L, L), 0)
    c_i = lax.broadcasted_iota(jnp.int32, (L, L), 1)
    causal = r_i >= c_i
    scale = dh ** -0.5
    heads = []
    for h in range(MLSTM_HEADS):
        q = mproj[rows, h * dh:(h + 1) * dh].astype(BF16)
        k = mproj[rows, MLSTM_WIDTH + h * dh:MLSTM_WIDTH + (h + 1) * dh] * scale
        state = s_ref[h]
        qk = _dot_nt(q, k.astype(BF16))
        from_state = _dot(q, state.astype(BF16))
        heads.append(dict(k=k, state=state, qk=qk, from_state=from_state))
    for h, hd in enumerate(heads):
        b_col = t_col[:, MLSTM_HEADS + h:MLSTM_HEADS + h + 1]
        d_col = t_col[:, h:h + 1]
        d_row = t_row[h:h + 1, :]
        m_prev = m_ref[h][0:1, 0:1]
        logw = jnp.where(causal, b_col + d_row, NEG_INF)
        inter = b_col + m_prev
        m_t = jnp.maximum(inter, jnp.max(logw, axis=-1, keepdims=True))
        scores = hd["qk"] * jnp.exp(logw - m_t)
        m_new = m_t[L - 1:L, :]
        b_last = b_col[L - 1:L, :]
        ws = jnp.exp(b_last + d_col - m_new)
        hd.update(m_t=m_t, w_inter=jnp.exp(inter - m_t), scores=scores,
                  row_sum=jnp.sum(scores, axis=-1, keepdims=True),
                  decay=jnp.exp(b_last + m_prev - m_new), m_new=m_new,
                  kw=(hd["k"] * ws).astype(BF16))
    return heads


def _mlstm_stage2(mproj, heads, s_ref, m_ref, nw_ref, row0):
    L = MLSTM_CHUNK
    dh = MLSTM_HEAD_DIM
    rows = slice(row0, row0 + L)
    lane = lax.broadcasted_iota(jnp.int32, (1, LANES), 1)
    ones_tile = jnp.broadcast_to(jnp.where(lane == 0, 1.0, 0.0).astype(BF16), (L, LANES))
    outs = []
    for h, hd in enumerate(heads):
        v = mproj[rows, 2 * MLSTM_WIDTH + h * dh:2 * MLSTM_WIDTH + (h + 1) * dh].astype(BF16)
        hd["intra"] = _dot(hd["scores"].astype(BF16), v)
        v_ext = jnp.concatenate([v, ones_tile], axis=1)
        s_ref[h] = hd["decay"] * hd["state"] + _dot_tn(hd["kw"], v_ext)
        m_ref[h] = jnp.broadcast_to(hd["m_new"], (SUBLANES, LANES))
    for h, hd in enumerate(heads):
        og = mproj[rows, 3 * MLSTM_WIDTH + h * dh:3 * MLSTM_WIDTH + (h + 1) * dh]
        num = hd["w_inter"] * hd["from_state"][:, 0:dh] + hd["intra"]
        den = hd["w_inter"] * hd["from_state"][:, dh:dh + 1] + hd["row_sum"]
        hh = num / jnp.maximum(jnp.abs(den), jnp.exp(-hd["m_t"]))
        mu = jnp.mean(hh, axis=-1, keepdims=True)
        hc = hh - mu
        hn = hc * lax.rsqrt(jnp.mean(hc * hc, axis=-1, keepdims=True) + LN_EPS)
        outs.append(hn * nw_ref[:, h * dh:(h + 1) * dh] * _sigmoid(og))
    return outs


def _mixer_kernel(sinks_ref, x_ref, pos_ref, tab_ref, rot_ref, win_ref, bin_ref, nw_ref, cw_ref,
                  wb_ref, wo_ref, lnw_ref, lnb_ref, o_ref,
                  kprev_ref, vprev_ref, s_ref, m_ref, conv_ref, y_ref, *, tm, tsub):
    s_idx = pl.program_id(1)

    @pl.when(s_idx == 0)
    def _reset():
        kprev_ref[...] = jnp.zeros_like(kprev_ref)
        vprev_ref[...] = jnp.zeros_like(vprev_ref)
        s_ref[...] = jnp.zeros_like(s_ref)
        m_ref[...] = jnp.zeros_like(m_ref)
        conv_ref[0:SUBLANES, :] = jnp.zeros((SUBLANES, CONV_WIDTH), F32)

    kv = (kprev_ref[...], vprev_ref[...])
    for sub in range(tm // tsub):
        kv = _mixer_subtile(sub, kv, s_idx, sinks_ref, x_ref, pos_ref, tab_ref, rot_ref, win_ref, bin_ref,
                            nw_ref, cw_ref, wb_ref, wo_ref, lnw_ref, lnb_ref, o_ref,
                            s_ref, m_ref, conv_ref, y_ref, tsub)
    kprev_ref[...], vprev_ref[...] = kv
    conv_ref[0:SUBLANES, :] = conv_ref[tm:tm + SUBLANES, :]


def _mixer_subtile(sub, kv, s_idx, sinks_ref, x_ref, pos_ref, tab_ref, rot_ref, win_ref, bin_ref,
                   nw_ref, cw_ref, wb_ref, wo_ref, lnw_ref, lnb_ref, o_ref,
                   s_ref, m_ref, conv_ref, y_ref, tsub):
    L = ATTN_BLOCK
    nblk = tsub // L
    ngrp = N_Q_HEADS // 2
    r0 = sub * tsub
    x = x_ref[0, r0:r0 + tsub, :]
    xb = x.astype(BF16)

    def proj(lo, hi):
        return _dot(xb, win_ref[:, lo:hi]) + bin_ref[:, lo:hi]

    gates = []

    def fill():
        b = len(gates)
        if b < N_BRANCH:
            gates.append(_sigmoid(proj(OFF_G + b * D_MODEL, OFF_G + (b + 1) * D_MODEL)))

    ropes = [_rope_tables(pos_ref[0, sub * nblk + j:sub * nblk + j + 1, :], tab_ref[...], rot_ref[...])
             for j in range(nblk)]
    qkv = proj(OFF_AQ, OFF_M)
    gif = proj(OFF_IF, P_PACKED)
    mproj = proj(OFF_M, OFF_C)
    mgates = [_mlstm_gates(gif, j * L) for j in range(nblk)]
    cproj = proj(OFF_C, OFF_G)

    u = cproj[:, 2 * CONV_WIDTH:] * cproj[:, 0:CONV_WIDTH]
    conv_ref[SUBLANES + r0:SUBLANES + r0 + tsub, :] = u
    u1 = conv_ref[pl.ds(SUBLANES + r0 - 1, tsub), :]
    u2 = conv_ref[pl.ds(SUBLANES + r0 - 2, tsub), :]
    yc = cw_ref[0:1, :] * u2 + cw_ref[1:2, :] * u1 + cw_ref[2:3, :] * u
    y_ref[r0:r0 + tsub, 2 * BRANCH_WIDTH:] = (cproj[:, CONV_WIDTH:2 * CONV_WIDTH] * yc).astype(BF16)

    k_prev, v_prev = kv
    for j in range(nblk):
        rows = slice(j * L, (j + 1) * L)
        yrows = slice(r0 + j * L, r0 + (j + 1) * L)
        cos_f, sin_f = ropes[j]
        q_all = jnp.concatenate(
            [(_rope(qkv[rows, c * LANES:(c + 1) * LANES], cos_f, sin_f)
              * (HEAD_DIM ** -0.5)).astype(BF16) for c in range(ngrp)], axis=0)
        k_cur = _rope(qkv[rows, OFF_AK:OFF_AK + LANES], cos_f, sin_f).astype(BF16)
        v_cur = qkv[rows, OFF_AV:OFF_AV + LANES].astype(BF16)
        thr = jnp.where(s_idx == 0, L, 0) if (sub == 0 and j == 0) else 0
        s_all = _attention_scores(q_all, k_cur, k_prev)
        heads = _mlstm_stage1(mproj, mgates[j][0], mgates[j][1], s_ref, m_ref, j * L)
        p_all, o_scale = _attention_softmax(s_all, thr, sinks_ref)
        fill()
        o_all = _dot(p_all, _split_kv(v_prev, v_cur)) * o_scale
        for c in range(ngrp):
            y_ref[yrows, c * LANES:(c + 1) * LANES] = o_all[c * L:(c + 1) * L, :].astype(BF16)
        outs = _mlstm_stage2(mproj, heads, s_ref, m_ref, nw_ref, j * L)
        for h, o in enumerate(outs):
            y_ref[yrows, BRANCH_WIDTH + h * MLSTM_HEAD_DIM:
                  BRANCH_WIDTH + (h + 1) * MLSTM_HEAD_DIM] = o.astype(BF16)
        k_prev, v_prev = k_cur, v_cur
        fill()
    while len(gates) < N_BRANCH:
        fill()

    merged = None
    for b in (2, 0, 1):
        term = gates[b] * _dot(y_ref[r0:r0 + tsub, b * BRANCH_WIDTH:(b + 1) * BRANCH_WIDTH], wb_ref[b])
        merged = term if merged is None else merged + term
    mix = _dot(merged.astype(BF16), wo_ref[...])
    y = ALPHA * x + mix
    o_ref[0, r0:r0 + tsub, :] = _layer_norm_rows(y, lnw_ref[...], lnb_ref[...])
    return k_prev, v_prev


ROPE_K = 8 * SUBLANES


def _rope_constants():
    half = ROT_DIM // 2
    inv_freq = ROPE_THETA ** (-np.arange(0, ROT_DIM, 2, dtype=np.float64) / ROT_DIM)
    freq = np.repeat(inv_freq[:, None], LANES, axis=1).astype(np.float32)
    l64 = np.arange(LANES) % HEAD_DIM
    expand = np.zeros((ROPE_K, 2 * LANES), np.float32)
    for f in range(half):
        cos_lanes = np.where((l64 < ROT_DIM) & (l64 % half == f), 1.0, 0.0)
        sin_lanes = np.where(l64 == f, -1.0, 0.0) + np.where(l64 == f + half, 1.0, 0.0)
        for piece in range(3):
            expand[piece * SUBLANES + f, :LANES] = cos_lanes
            expand[(3 + piece) * SUBLANES + f, LANES:] = sin_lanes
    expand[6 * SUBLANES, :LANES] = np.where(l64 >= ROT_DIM, 1.0, 0.0)
    return jnp.asarray(freq), jnp.asarray(expand, dtype=BF16)


def _pack_in_proj(w):
    offs = np.concatenate([[0], np.cumsum(SIZES)]).tolist()
    seg = [w[..., offs[i]:offs[i + 1]] for i in range(len(SIZES))]
    (aq, ak, av, mq, mk, mv, mo, mi, mf, cx, cb, cc, gates) = seg
    aq_heads = [aq[..., h * HEAD_DIM:(h + 1) * HEAD_DIM] for h in Q_HEAD_ORDER]
    zi = jnp.zeros(w.shape[:-1] + (LANES - MLSTM_HEADS,), w.dtype)
    zf = jnp.zeros(w.shape[:-1] + (LANES - 2 * MLSTM_HEADS,), w.dtype)
    return jnp.concatenate(aq_heads + [ak, av, mq, mk, mv, mo, cx, cb, cc, gates,
                                       mi, zi, mf, mf, zf], axis=-1)


def _pack_branch(w_branch):
    wb0 = jnp.concatenate([w_branch[:, 0:1, h * HEAD_DIM:(h + 1) * HEAD_DIM, :] for h in Q_HEAD_ORDER],
                          axis=2)
    return jnp.concatenate([wb0, w_branch[:, 1:]], axis=1)


def _mixer_tile(seq):
    if seq % 512 == 0:
        return 512, 256
    if seq % 256 == 0:
        return 256, 256
    return ATTN_BLOCK, ATTN_BLOCK


def _mixer_layer(x, layer, positions, freq, expand, win, bin_, sinks, norm_w, conv_w, wb, wo, ln_w, ln_b):
    bsz, seq, d = x.shape
    tm, tsub = _mixer_tile(seq)
    ns = seq // tm
    nblk = tm // ATTN_BLOCK
    pos = positions.reshape(bsz * ns, nblk, ATTN_BLOCK)

    def const(shape, idx):
        return pl.BlockSpec(shape, lambda b, s, *_: idx, pipeline_mode=pl.Buffered(1))

    grid_spec = pltpu.PrefetchScalarGridSpec(
        num_scalar_prefetch=1,
        grid=(bsz, ns),
        in_specs=[
            pl.BlockSpec((1, tm, d), lambda b, s, *_: (b, s, 0)),
            pl.BlockSpec((1, nblk, ATTN_BLOCK), lambda b, s, *_: (b * ns + s, 0, 0)),
            const((SUBLANES, LANES), (0, 0)),
            const((ROPE_K, 2 * LANES), (0, 0)),
            const((None, d, P_PACKED), (layer, 0, 0)),
            const((None, 1, P_PACKED), (layer, 0, 0)),
            const((None, 1, MLSTM_WIDTH), (layer, 0, 0)),
            const((None, CONV_K, CONV_WIDTH), (layer, 0, 0)),
            const((None, N_BRANCH, BRANCH_WIDTH, d), (layer, 0, 0, 0)),
            const((None, d, d), (layer, 0, 0)),
            const((None, None, 1, d), (layer, 1, 0, 0)),
            const((None, None, 1, d), (layer, 1, 0, 0)),
        ],
        out_specs=pl.BlockSpec((1, tm, d), lambda b, s, *_: (b, s, 0)),
        scratch_shapes=[
            pltpu.VMEM((ATTN_BLOCK, LANES), BF16),
            pltpu.VMEM((ATTN_BLOCK, LANES), BF16),
            pltpu.VMEM((MLSTM_HEADS, MLSTM_HEAD_DIM, 2 * MLSTM_HEAD_DIM), F32),
            pltpu.VMEM((MLSTM_HEADS, SUBLANES, LANES), F32),
            pltpu.VMEM((tm + 2 * SUBLANES, CONV_WIDTH), F32),
            pltpu.VMEM((tm, N_BRANCH * BRANCH_WIDTH), BF16),
        ],
    )
    return pl.pallas_call(
        functools.partial(_mixer_kernel, tm=tm, tsub=tsub),
        out_shape=jax.ShapeDtypeStruct((bsz, seq, d), F32),
        grid_spec=grid_spec,
        compiler_params=pltpu.CompilerParams(
            dimension_semantics=("arbitrary", "arbitrary"),
            vmem_limit_bytes=VMEM_LIMIT_BYTES),
        name="token_mixer",
    )(sinks, x, pos, freq, expand, win, bin_, norm_w, conv_w, wb, wo, ln_w, ln_b)


def kernel(x, positions, w_in, b_in, attn_sinks, mlstm_norm_w, conv_w, w_branch, w_out,
           ffn1_w_gu, ffn1_w_down, ffn2_w_gu, ffn2_w_down, ln_w, ln_b):
    bsz, seq, d = x.shape
    depth = w_in.shape[0]
    ffn_w = ((ffn1_w_gu.astype(BF16), ffn1_w_down.astype(BF16)),
             (ffn2_w_gu.astype(BF16), ffn2_w_down.astype(BF16)))
    win = _pack_in_proj(w_in.astype(BF16))
    bin_ = _pack_in_proj(b_in)[:, None, :]
    wb = _pack_branch(w_branch.astype(BF16))
    wo = w_out.astype(BF16)
    norm_w = mlstm_norm_w[:, None, :]
    ln_w4 = ln_w[:, :, None, :]
    ln_b4 = ln_b[:, :, None, :]
    freq, expand = _rope_constants()
    for l in range(depth):
        x = _ffn_layer(x.reshape(bsz * seq, d), l, ffn_w[0][0], ffn_w[0][1], ln_w4, ln_b4, 0
                       ).reshape(bsz, seq, d)
        x = _mixer_layer(x, l, positions, freq, expand, win, bin_, attn_sinks[l], norm_w, conv_w, wb, wo,
                         ln_w4, ln_b4)
        x = _ffn_layer(x.reshape(bsz * seq, d), l, ffn_w[1][0], ffn_w[1][1], ln_w4, ln_b4, 2
                       ).reshape(bsz, seq, d)
    return x
```

```python
import functools
import math

import numpy as np
import jax
import jax.numpy as jnp
from jax import lax
from jax.experimental import pallas as pl
from jax.experimental.pallas import tpu as pltpu

D_MODEL = 1024
DEPTH = 2

HEAD_DIM = 64
N_Q_HEADS = 8
N_KV_HEADS = 2
GROUP = N_Q_HEADS // N_KV_HEADS
WINDOW = 128
ATTN_BLOCK = 128
ROPE_THETA = 500000.0
ROT_DIM = HEAD_DIM // 4
ATTN_WIDTH = N_Q_HEADS * HEAD_DIM
KV_WIDTH = N_KV_HEADS * HEAD_DIM

MLSTM_HEADS = 4
MLSTM_HEAD_DIM = D_MODEL // 8
MLSTM_WIDTH = MLSTM_HEADS * MLSTM_HEAD_DIM
MLSTM_CHUNK = 128

CONV_WIDTH = D_MODEL // 2
CONV_K = 3

N_BRANCH = 3
BRANCH_WIDTH = 512
D_FF = 2816
LN_EPS = 1e-5
ALPHA = (2.0 * DEPTH) ** 0.25
NEG_INF = -1e30

SIZES = (ATTN_WIDTH, KV_WIDTH, KV_WIDTH,
         MLSTM_WIDTH, MLSTM_WIDTH, MLSTM_WIDTH, MLSTM_WIDTH, MLSTM_HEADS, MLSTM_HEADS,
         CONV_WIDTH, CONV_WIDTH, CONV_WIDTH,
         N_BRANCH * D_MODEL)

LANES = 128
SUBLANES = 8
VMEM_LIMIT_BYTES = 56 * 1024 * 1024

OFF_AQ = 0
OFF_AK = OFF_AQ + ATTN_WIDTH
OFF_AV = OFF_AK + KV_WIDTH
OFF_M = OFF_AV + KV_WIDTH
OFF_C = OFF_M + 4 * MLSTM_WIDTH
OFF_G = OFF_C + 3 * CONV_WIDTH
OFF_IF = OFF_G + N_BRANCH * D_MODEL
P_PACKED = OFF_IF + 2 * LANES
GATE_PART = D_MODEL

Q_HEAD_ORDER = (0, 4, 1, 5, 2, 6, 3, 7)

BF16 = jnp.bfloat16
F32 = jnp.float32


def _dot(a, b):
    return jnp.dot(a, b, preferred_element_type=F32)


def _dot_nt(a, b):
    return lax.dot_general(a, b, (((1,), (1,)), ((), ())), preferred_element_type=F32)


def _dot_tn(a, b):
    return lax.dot_general(a, b, (((0,), (0,)), ((), ())), preferred_element_type=F32)


def _layer_norm_rows(y, w, b):
    mu = jnp.mean(y, axis=-1, keepdims=True)
    yc = y - mu
    var = jnp.mean(yc * yc, axis=-1, keepdims=True)
    return yc * lax.rsqrt(var + LN_EPS) * w + b


def _sigmoid(x):
    return 0.5 * jnp.tanh(0.5 * x) + 0.5


def _log_sigmoid(x):
    return jnp.minimum(x, 0.0) - jnp.log1p(jnp.exp(-jnp.abs(x)))


def _ffn_kernel(x_ref, wgu_ref, wd_ref, lnw_ref, lnb_ref, o_ref, *, chunks):
    d_ff = wd_ref.shape[0]
    x = x_ref[...]
    xb = x.astype(BF16)
    acc = None
    for c0, c1 in chunks:
        g = _dot(xb, wgu_ref[:, c0:c1])
        u = _dot(xb, wgu_ref[:, d_ff + c0:d_ff + c1])
        a = (g * _sigmoid(g) * u).astype(BF16)
        part = _dot(a, wd_ref[c0:c1, :])
        acc = part if acc is None else acc + part
    y = ALPHA * x + 0.5 * acc
    o_ref[...] = _layer_norm_rows(y, lnw_ref[...], lnb_ref[...])


def _ffn_tiles(n_tokens, d_ff):
    tm = 512 if n_tokens % 512 == 0 else 128
    step = 2 * LANES
    chunks = tuple((c, min(c + step, d_ff)) for c in range(0, d_ff, step))
    return tm, chunks


def _ffn_layer(x2d, layer, w_gu, w_down, ln_w, ln_b, ln_idx):
    n, d = x2d.shape
    d_ff = w_down.shape[1]
    tm, chunks = _ffn_tiles(n, d_ff)

    def const(shape, idx):
        return pl.BlockSpec(shape, lambda i: idx, pipeline_mode=pl.Buffered(1))

    return pl.pallas_call(
        functools.partial(_ffn_kernel, chunks=chunks),
        out_shape=jax.ShapeDtypeStruct((n, d), F32),
        grid=(n // tm,),
        in_specs=[
            pl.BlockSpec((tm, d), lambda i: (i, 0)),
            const((None, d, 2 * d_ff), (layer, 0, 0)),
            const((None, d_ff, d), (layer, 0, 0)),
            const((None, None, 1, d), (layer, ln_idx, 0, 0)),
            const((None, None, 1, d), (layer, ln_idx, 0, 0)),
        ],
        out_specs=pl.BlockSpec((tm, d), lambda i: (i, 0)),
        compiler_params=pltpu.CompilerParams(
            dimension_semantics=("arbitrary",),
            vmem_limit_bytes=VMEM_LIMIT_BYTES),
        name="swiglu_ln",
    )(x2d, w_gu, w_down, ln_w, ln_b)


def _rope_tables(pos_row, freq, expand):
    ang = freq * pos_row.astype(F32)
    pieces = []
    for t in (jnp.cos(ang), jnp.sin(ang)):
        hi = t.astype(BF16)
        r1 = t - hi.astype(F32)
        mid = r1.astype(BF16)
        lo = (r1 - mid.astype(F32)).astype(BF16)
        pieces += [hi, mid, lo]
    pieces += [jnp.ones((SUBLANES, LANES), BF16), jnp.zeros((SUBLANES, LANES), BF16)]
    cs = _dot_tn(jnp.concatenate(pieces, axis=0), expand)
    return cs[:, 0:LANES], cs[:, LANES:]


def _rope(t, cos_f, sin_f):
    half = ROT_DIM // 2
    lane = lax.broadcasted_iota(jnp.int32, (1, LANES), 1)
    up = pltpu.roll(t, LANES - half, axis=1)
    down = pltpu.roll(t, half, axis=1)
    return t * cos_f + jnp.where((lane & (HEAD_DIM - 1)) < half, up, down) * sin_f


def _split_kv(prev, cur):
    lane = lax.broadcasted_iota(jnp.int32, (1, LANES), 1)
    half0 = lane < HEAD_DIM
    zero = jnp.zeros((), BF16)
    both = jnp.concatenate([prev, cur], axis=0)
    return jnp.concatenate([jnp.where(half0, both, zero),
                            jnp.where(half0, zero, both)], axis=0)


def _attention_scores(q_all, k_cur, k_prev):
    return _dot_nt(q_all, _split_kv(k_prev, k_cur))


def _attention_softmax(s_all, thr, sinks_ref):
    L = ATTN_BLOCK
    lane = lax.broadcasted_iota(jnp.int32, (1, LANES), 1)
    half0 = lane < HEAD_DIM
    qi = lax.broadcasted_iota(jnp.int32, (L, 2 * L), 0)
    kj = lax.broadcasted_iota(jnp.int32, (L, 2 * L), 1)
    rel = qi + L - kj
    mask = (rel >= 0) & (rel < WINDOW) & (kj >= thr)
    ps, scales = [], []
    for c in range(N_Q_HEADS // 2):
        pg, inv = [], []
        for g in range(N_KV_HEADS):
            sink = sinks_ref[Q_HEAD_ORDER[2 * c + g]]
            sg = jnp.where(mask, s_all[c * L:(c + 1) * L, g * 2 * L:(g + 1) * 2 * L], NEG_INF)
            m = jnp.maximum(jnp.max(sg, axis=-1, keepdims=True), sink)
            p = jnp.exp(sg - m)
            den = jnp.sum(p, axis=-1, keepdims=True) + jnp.exp(sink - m)
            pg.append(p.astype(BF16))
            inv.append(1.0 / den)
        ps.append(jnp.concatenate(pg, axis=1))
        scales.append(jnp.where(half0, inv[0], inv[1]))
    return jnp.concatenate(ps, axis=0), jnp.concatenate(scales, axis=0)


def _mlstm_gates(gif, row0):
    L = MLSTM_CHUNK
    rows = slice(row0, row0 + L)
    lane = lax.broadcasted_iota(jnp.int32, (1, LANES), 1)
    gi = gif[rows, 0:LANES]
    gf = gif[rows, LANES:2 * LANES]
    lf = jnp.where(lane < 2 * MLSTM_HEADS, _log_sigmoid(gf), 0.0)
    r_i = lax.broadcasted_iota(jnp.int32, (L, L), 0)
    c_i = lax.broadcasted_iota(jnp.int32, (L, L), 1)
    tril = jnp.where(r_i >= c_i, 1.0, 0.0).astype(BF16)
    hi = lf.astype(BF16)
    r1 = lf - hi.astype(F32)
    mid = r1.astype(BF16)
    lo = (r1 - mid.astype(F32)).astype(BF16)
    bsum = _dot(tril, jnp.concatenate([hi, mid, lo], axis=1))
    bsum = bsum[:, 0:LANES] + bsum[:, LANES:2 * LANES] + bsum[:, 2 * LANES:]
    t_col = jnp.where(lane < MLSTM_HEADS, gi - bsum, bsum)
    return t_col, t_col.T


def _mlstm_stage1(mqk, t_col, t_row, s_ref, m_ref, row0):
    L = MLSTM_CHUNK
    dh = MLSTM_HEAD_DIM
    rows = slice(row0, row0 + L)
    r_i = lax.broadcasted_iota(jnp.int32, (L, L), 0)
    c_i = lax.broadcasted_iota(jnp.int32, (L, L), 1)
    causal = r_i >= c_i
    scale = dh ** -0.5
    heads = []
    for h in range(MLSTM_HEADS):
        q = mqk[rows, h * dh:(h + 1) * dh].astype(BF16)
        k = mqk[rows, MLSTM_WIDTH + h * dh:MLSTM_WIDTH + (h + 1) * dh] * scale
        state = s_ref[h]
        qk = _dot_nt(q, k.astype(BF16))
        from_state = _dot(q, state.astype(BF16))
        heads.append(dict(k=k, state=state, qk=qk, from_state=from_state))
    for h, hd in enumerate(heads):
        b_col = t_col[:, MLSTM_HEADS + h:MLSTM_HEADS + h + 1]
        d_col = t_col[:, h:h + 1]
        d_row = t_row[h:h + 1, :]
        m_prev = m_ref[h][0:1, 0:1]
        logw = jnp.where(causal, b_col + d_row, NEG_INF)
        inter = b_col + m_prev
        m_t = jnp.maximum(inter, jnp.max(logw, axis=-1, keepdims=True))
        scores = hd["qk"] * jnp.exp(logw - m_t)
        m_new = m_t[L - 1:L, :]
        b_last = b_col[L - 1:L, :]
        ws = jnp.exp(b_last + d_col - m_new)
        hd.update(m_t=m_t, w_inter=jnp.exp(inter - m_t), scores=scores,
                  row_sum=jnp.sum(scores, axis=-1, keepdims=True),
                  decay=jnp.exp(b_last + m_prev - m_new), m_new=m_new,
                  kw=(hd["k"] * ws).astype(BF16))
    return heads


def _mlstm_stage2(mvo, heads, s_ref, m_ref, nw_ref, row0):
    L = MLSTM_CHUNK
    dh = MLSTM_HEAD_DIM
    rows = slice(row0, row0 + L)
    lane = lax.broadcasted_iota(jnp.int32, (1, LANES), 1)
    ones_tile = jnp.broadcast_to(jnp.where(lane == 0, 1.0, 0.0).astype(BF16), (L, LANES))
    outs = []
    for h, hd in enumerate(heads):
        v = mvo[rows, h * dh:(h + 1) * dh].astype(BF16)
        hd["intra"] = _dot(hd["scores"].astype(BF16), v)
        v_ext = jnp.concatenate([v, ones_tile], axis=1)
        s_ref[h] = hd["decay"] * hd["state"] + _dot_tn(hd["kw"], v_ext)
        m_ref[h] = jnp.broadcast_to(hd["m_new"], (SUBLANES, LANES))
    for h, hd in enumerate(heads):
        og = mvo[rows, MLSTM_WIDTH + h * dh:MLSTM_WIDTH + (h + 1) * dh]
        num = hd["w_inter"] * hd["from_state"][:, 0:dh] + hd["intra"]
        den = hd["w_inter"] * hd["from_state"][:, dh:dh + 1] + hd["row_sum"]
        hh = num / jnp.maximum(jnp.abs(den), jnp.exp(-hd["m_t"]))
        mu = jnp.mean(hh, axis=-1, keepdims=True)
        hc = hh - mu
        hn = hc * lax.rsqrt(jnp.mean(hc * hc, axis=-1, keepdims=True) + LN_EPS)
        outs.append(hn * nw_ref[:, h * dh:(h + 1) * dh] * _sigmoid(og))
    return outs


class _SubTile:
    def __init__(self, sub, tsub, s_idx, refs):
        self.sub, self.tsub, self.s_idx, self.r = sub, tsub, s_idx, refs
        self.r0 = sub * tsub
        self.nblk = tsub // ATTN_BLOCK
        self.x = refs["x"][0, self.r0:self.r0 + tsub, :]
        self.xb = self.x.astype(BF16)
        self.gates = [[None] * (D_MODEL // GATE_PART) for _ in range(N_BRANCH)]
        self.merged = None

    def proj(self, lo, hi):
        return _dot(self.xb, self.r["win"][:, lo:hi]) + self.r["bin"][:, lo:hi]

    def ropes(self):
        r, n = self.r, self.nblk
        self.rope = [_rope_tables(r["pos"][0, self.sub * n + j:self.sub * n + j + 1, :],
                                  r["freq"][...], r["expand"][...]) for j in range(n)]

    def proj_attn(self):
        self.qkv = self.proj(OFF_AQ, OFF_M)

    def proj_gates(self):
        gif = self.proj(OFF_IF, P_PACKED)
        self.mgates = [_mlstm_gates(gif, j * MLSTM_CHUNK) for j in range(self.nblk)]

    def proj_mlstm_qk(self):
        self.mqk = self.proj(OFF_M, OFF_M + 2 * MLSTM_WIDTH)

    def proj_mlstm_vo(self):
        self.mvo = self.proj(OFF_M + 2 * MLSTM_WIDTH, OFF_C)

    def conv(self):
        r, r0, tsub = self.r, self.r0, self.tsub
        cproj = self.proj(OFF_C, OFF_G)
        u = cproj[:, 2 * CONV_WIDTH:] * cproj[:, 0:CONV_WIDTH]
        r["conv"][SUBLANES + r0:SUBLANES + r0 + tsub, :] = u
        u1 = r["conv"][pl.ds(SUBLANES + r0 - 1, tsub), :]
        u2 = r["conv"][pl.ds(SUBLANES + r0 - 2, tsub), :]
        yc = r["cw"][0:1, :] * u2 + r["cw"][1:2, :] * u1 + r["cw"][2:3, :] * u
        r["y"][r0:r0 + tsub, 2 * BRANCH_WIDTH:] = (cproj[:, CONV_WIDTH:2 * CONV_WIDTH] * yc).astype(BF16)

    def gate(self, b, part):
        lo = OFF_G + b * D_MODEL + part * GATE_PART
        self.gates[b][part] = _sigmoid(self.proj(lo, lo + GATE_PART))

    def stage1(self, j, kv):
        L = ATTN_BLOCK
        rows = slice(j * L, (j + 1) * L)
        cos_f, sin_f = self.rope[j]
        q_all = jnp.concatenate(
            [(_rope(self.qkv[rows, c * LANES:(c + 1) * LANES], cos_f, sin_f)
              * (HEAD_DIM ** -0.5)).astype(BF16) for c in range(N_Q_HEADS // 2)], axis=0)
        self.k_cur = _rope(self.qkv[rows, OFF_AK:OFF_AK + LANES], cos_f, sin_f).astype(BF16)
        self.v_cur = self.qkv[rows, OFF_AV:OFF_AV + LANES].astype(BF16)
        thr = jnp.where(self.s_idx == 0, L, 0) if (self.sub == 0 and j == 0) else 0
        s_all = _attention_scores(q_all, self.k_cur, kv[0])
        self.heads = _mlstm_stage1(self.mqk, self.mgates[j][0], self.mgates[j][1],
                                   self.r["s"], self.r["m"], j * L)
        self.p_all, self.o_scale = _attention_softmax(s_all, thr, self.r["sinks"])

    def stage2(self, j, kv):
        L = ATTN_BLOCK
        r = self.r
        yrows = slice(self.r0 + j * L, self.r0 + (j + 1) * L)
        o_all = _dot(self.p_all, _split_kv(kv[1], self.v_cur)) * self.o_scale
        for c in range(N_Q_HEADS // 2):
            r["y"][yrows, c * LANES:(c + 1) * LANES] = o_all[c * L:(c + 1) * L, :].astype(BF16)
        outs = _mlstm_stage2(self.mvo, self.heads, r["s"], r["m"], r["nw"], j * L)
        for h, o in enumerate(outs):
            r["y"][yrows, BRANCH_WIDTH + h * MLSTM_HEAD_DIM:
                   BRANCH_WIDTH + (h + 1) * MLSTM_HEAD_DIM] = o.astype(BF16)
        return self.k_cur, self.v_cur

    def branch(self, b):
        r, r0, tsub = self.r, self.r0, self.tsub
        term = jnp.concatenate(self.gates[b], axis=1) * _dot(
            r["y"][r0:r0 + tsub, b * BRANCH_WIDTH:(b + 1) * BRANCH_WIDTH], r["wb"][b])
        self.merged = term if self.merged is None else self.merged + term

    def finish(self):
        r = self.r
        mix = _dot(self.merged.astype(BF16), r["wo"][...])
        y = ALPHA * self.x + mix
        r["o"][0, self.r0:self.r0 + self.tsub, :] = _layer_norm_rows(y, r["lnw"][...], r["lnb"][...])


def _mixer_kernel(sinks_ref, x_ref, pos_ref, freq_ref, expand_ref, win_ref, bin_ref, nw_ref, cw_ref,
                  wb_ref, wo_ref, lnw_ref, lnb_ref, o_ref,
                  kprev_ref, vprev_ref, s_ref, m_ref, conv_ref, y_ref, *, tm, tsub):
    s_idx = pl.program_id(1)

    @pl.when(s_idx == 0)
    def _reset():
        kprev_ref[...] = jnp.zeros_like(kprev_ref)
        vprev_ref[...] = jnp.zeros_like(vprev_ref)
        s_ref[...] = jnp.zeros_like(s_ref)
        m_ref[...] = jnp.zeros_like(m_ref)
        conv_ref[0:SUBLANES, :] = jnp.zeros((SUBLANES, CONV_WIDTH), F32)

    refs = dict(sinks=sinks_ref, x=x_ref, pos=pos_ref, freq=freq_ref, expand=expand_ref, win=win_ref,
                bin=bin_ref, nw=nw_ref, cw=cw_ref, wb=wb_ref, wo=wo_ref, lnw=lnw_ref, lnb=lnb_ref,
                o=o_ref, s=s_ref, m=m_ref, conv=conv_ref, y=y_ref)
    nsub = tm // tsub
    tiles = [_SubTile(t, tsub, s_idx, refs) for t in range(nsub)]

    def head_steps(t):
        return [t.ropes, t.proj_attn, t.proj_gates, t.proj_mlstm_qk, t.proj_mlstm_vo]

    def tail_steps(t):
        return [lambda: t.branch(2), lambda: t.branch(0), lambda: t.branch(1), t.finish]

    for step in head_steps(tiles[0]):
        step()
    kv = (kprev_ref[...], vprev_ref[...])
    carry_tail = []
    for i, t in enumerate(tiles):
        own = [t.conv] + [functools.partial(t.gate, b, p) for b in range(N_BRANCH)
                          for p in range(D_MODEL // GATE_PART)]
        nxt = head_steps(tiles[i + 1]) if i + 1 < nsub else []
        fillers = carry_tail + own + nxt
        slots = 2 * t.nblk
        for slot in range(slots):
            j = slot // 2
            if slot % 2 == 0:
                t.stage1(j, kv)
            else:
                kv = t.stage2(j, kv)
            take = -(-len(fillers) // (slots - slot))
            for step in fillers[:take]:
                step()
            fillers = fillers[take:]
        carry_tail = tail_steps(t)
    for step in carry_tail:
        step()
    kprev_ref[...], vprev_ref[...] = kv
    conv_ref[0:SUBLANES, :] = conv_ref[tm:tm + SUBLANES, :]


ROPE_K = 8 * SUBLANES


def _rope_constants():
    half = ROT_DIM // 2
    inv_freq = ROPE_THETA ** (-np.arange(0, ROT_DIM, 2, dtype=np.float64) / ROT_DIM)
    freq = np.repeat(inv_freq[:, None], LANES, axis=1).astype(np.float32)
    l64 = np.arange(LANES) % HEAD_DIM
    expand = np.zeros((ROPE_K, 2 * LANES), np.float32)
    for f in range(half):
        cos_lanes = np.where((l64 < ROT_DIM) & (l64 % half == f), 1.0, 0.0)
        sin_lanes = np.where(l64 == f, -1.0, 0.0) + np.where(l64 == f + half, 1.0, 0.0)
        for piece in range(3):
            expand[piece * SUBLANES + f, :LANES] = cos_lanes
            expand[(3 + piece) * SUBLANES + f, LANES:] = sin_lanes
    expand[6 * SUBLANES, :LANES] = np.where(l64 >= ROT_DIM, 1.0, 0.0)
    return jnp.asarray(freq), jnp.asarray(expand, dtype=BF16)


def _pack_in_proj(w):
    offs = np.concatenate([[0], np.cumsum(SIZES)]).tolist()
    seg = [w[..., offs[i]:offs[i + 1]] for i in range(len(SIZES))]
    (aq, ak, av, mq, mk, mv, mo, mi, mf, cx, cb, cc, gates) = seg
    aq_heads = [aq[..., h * HEAD_DIM:(h + 1) * HEAD_DIM] for h in Q_HEAD_ORDER]
    zi = jnp.zeros(w.shape[:-1] + (LANES - MLSTM_HEADS,), w.dtype)
    zf = jnp.zeros(w.shape[:-1] + (LANES - 2 * MLSTM_HEADS,), w.dtype)
    return jnp.concatenate(aq_heads + [ak, av, mq, mk, mv, mo, cx, cb, cc, gates,
                                       mi, zi, mf, mf, zf], axis=-1)


def _pack_kernel(w_ref, o_ref):
    offs = np.concatenate([[0], np.cumsum(SIZES)]).tolist()
    rows = w_ref.shape[0]
    lane = lax.broadcasted_iota(jnp.int32, (1, LANES), 1)
    for c in range(N_Q_HEADS // 2):
        pair = [w_ref[:, h * HEAD_DIM:(h + 1) * HEAD_DIM] for h in Q_HEAD_ORDER[2 * c:2 * c + 2]]
        o_ref[:, c * LANES:(c + 1) * LANES] = jnp.concatenate(pair, axis=1).astype(BF16)
    o_ref[:, OFF_AK:OFF_C] = w_ref[:, offs[1]:offs[7]].astype(BF16)
    o_ref[:, OFF_C:OFF_IF] = w_ref[:, offs[9]:offs[13]].astype(BF16)
    t = w_ref[:, offs[7]:offs[7] + LANES]
    o_ref[:, OFF_IF:OFF_IF + LANES] = jnp.where(lane < MLSTM_HEADS, t, 0.0).astype(BF16)
    f_lo = pltpu.roll(t, LANES - MLSTM_HEADS, axis=1)
    f_tile = jnp.where(lane < MLSTM_HEADS, f_lo, jnp.where(lane < 2 * MLSTM_HEADS, t, 0.0))
    o_ref[:, OFF_IF + LANES:P_PACKED] = f_tile.astype(BF16)


def _pack_in_proj_weights(w_in):
    depth, d, p_in = w_in.shape
    rows = 256
    return pl.pallas_call(
        _pack_kernel,
        out_shape=jax.ShapeDtypeStruct((depth, d, P_PACKED), BF16),
        grid=(depth, d // rows),
        in_specs=[pl.BlockSpec((None, rows, p_in), lambda l, i: (l, i, 0))],
        out_specs=pl.BlockSpec((None, rows, P_PACKED), lambda l, i: (l, i, 0)),
        compiler_params=pltpu.CompilerParams(
            dimension_semantics=("arbitrary", "arbitrary"),
            vmem_limit_bytes=VMEM_LIMIT_BYTES),
        name="pack_in_proj",
    )(w_in)


def _pack_branch(w_branch):
    wb0 = jnp.concatenate([w_branch[:, 0:1, h * HEAD_DIM:(h + 1) * HEAD_DIM, :] for h in Q_HEAD_ORDER],
                          axis=2)
    return jnp.concatenate([wb0, w_branch[:, 1:]], axis=1)


def _mixer_tile(seq):
    if seq % 512 == 0:
        return 512, 256
    if seq % 256 == 0:
        return 256, 256
    return ATTN_BLOCK, ATTN_BLOCK


def _mixer_layer(x, layer, positions, freq, expand, win, bin_, sinks, norm_w, conv_w, wb, wo, ln_w, ln_b):
    bsz, seq, d = x.shape
    tm, tsub = _mixer_tile(seq)
    ns = seq // tm
    nblk = tm // ATTN_BLOCK
    pos = positions.reshape(bsz * ns, nblk, ATTN_BLOCK)

    def const(shape, idx):
        return pl.BlockSpec(shape, lambda b, s, *_: idx, pipeline_mode=pl.Buffered(1))

    grid_spec = pltpu.PrefetchScalarGridSpec(
        num_scalar_prefetch=1,
        grid=(bsz, ns),
        in_specs=[
            pl.BlockSpec((1, tm, d), lambda b, s, *_: (b, s, 0)),
            pl.BlockSpec((1, nblk, ATTN_BLOCK), lambda b, s, *_: (b * ns + s, 0, 0)),
            const((SUBLANES, LANES), (0, 0)),
            const((ROPE_K, 2 * LANES), (0, 0)),
            const((None, d, P_PACKED), (layer, 0, 0)),
            const((None, 1, P_PACKED), (layer, 0, 0)),
            const((None, 1, MLSTM_WIDTH), (layer, 0, 0)),
            const((None, CONV_K, CONV_WIDTH), (layer, 0, 0)),
            const((None, N_BRANCH, BRANCH_WIDTH, d), (layer, 0, 0, 0)),
            const((None, d, d), (layer, 0, 0)),
            const((None, None, 1, d), (layer, 1, 0, 0)),
            const((None, None, 1, d), (layer, 1, 0, 0)),
        ],
        out_specs=pl.BlockSpec((1, tm, d), lambda b, s, *_: (b, s, 0)),
        scratch_shapes=[
            pltpu.VMEM((ATTN_BLOCK, LANES), BF16),
            pltpu.VMEM((ATTN_BLOCK, LANES), BF16),
            pltpu.VMEM((MLSTM_HEADS, MLSTM_HEAD_DIM, 2 * MLSTM_HEAD_DIM), F32),
            pltpu.VMEM((MLSTM_HEADS, SUBLANES, LANES), F32),
            pltpu.VMEM((tm + 2 * SUBLANES, CONV_WIDTH), F32),
            pltpu.VMEM((tm, N_BRANCH * BRANCH_WIDTH), BF16),
        ],
    )
    return pl.pallas_call(
        functools.partial(_mixer_kernel, tm=tm, tsub=tsub),
        out_shape=jax.ShapeDtypeStruct((bsz, seq, d), F32),
        grid_spec=grid_spec,
        compiler_params=pltpu.CompilerParams(
            dimension_semantics=("arbitrary", "arbitrary"),
            vmem_limit_bytes=VMEM_LIMIT_BYTES),
        name="token_mixer",
    )(sinks, x, pos, freq, expand, win, bin_, norm_w, conv_w, wb, wo, ln_w, ln_b)


def kernel(x, positions, w_in, b_in, attn_sinks, mlstm_norm_w, conv_w, w_branch, w_out,
           ffn1_w_gu, ffn1_w_down, ffn2_w_gu, ffn2_w_down, ln_w, ln_b):
    bsz, seq, d = x.shape
    depth = w_in.shape[0]
    ffn_w = ((ffn1_w_gu.astype(BF16), ffn1_w_down.astype(BF16)),
             (ffn2_w_gu.astype(BF16), ffn2_w_down.astype(BF16)))
    win = _pack_in_proj_weights(w_in)
    bin_ = _pack_in_proj(b_in)[:, None, :]
    wb = _pack_branch(w_branch.astype(BF16))
    wo = w_out.astype(BF16)
    norm_w = mlstm_norm_w[:, None, :]
    ln_w4 = ln_w[:, :, None, :]
    ln_b4 = ln_b[:, :, None, :]
    freq, expand = _rope_constants()
    for l in range(depth):
        x = _ffn_layer(x.reshape(bsz * seq, d), l, ffn_w[0][0], ffn_w[0][1], ln_w4, ln_b4, 0
                       ).reshape(bsz, seq, d)
        x = _mixer_layer(x, l, positions, freq, expand, win, bin_, attn_sinks[l], norm_w, conv_w, wb, wo,
                         ln_w4, ln_b4)
        x = _ffn_layer(x.reshape(bsz * seq, d), l, ffn_w[1][0], ffn_w[1][1], ln_w4, ln_b4, 2
                       ).reshape(bsz, seq, d)
    return x
```

```python
import functools
import math

import numpy as np
import jax
import jax.numpy as jnp
from jax import lax
from jax.experimental import pallas as pl
from jax.experimental.pallas import tpu as pltpu

D_MODEL = 1024
DEPTH = 2

HEAD_DIM = 64
N_Q_HEADS = 8
N_KV_HEADS = 2
GROUP = N_Q_HEADS // N_KV_HEADS
WINDOW = 128
ATTN_BLOCK = 128
ROPE_THETA = 500000.0
ROT_DIM = HEAD_DIM // 4
ATTN_WIDTH = N_Q_HEADS * HEAD_DIM
KV_WIDTH = N_KV_HEADS * HEAD_DIM

MLSTM_HEADS = 4
MLSTM_HEAD_DIM = D_MODEL // 8
MLSTM_WIDTH = MLSTM_HEADS * MLSTM_HEAD_DIM
MLSTM_CHUNK = 128

CONV_WIDTH = D_MODEL // 2
CONV_K = 3

N_BRANCH = 3
BRANCH_WIDTH = 512
D_FF = 2816
LN_EPS = 1e-5
ALPHA = (2.0 * DEPTH) ** 0.25
NEG_INF = -1e30
LOG2E = math.log2(math.e)

SIZES = (ATTN_WIDTH, KV_WIDTH, KV_WIDTH,
         MLSTM_WIDTH, MLSTM_WIDTH, MLSTM_WIDTH, MLSTM_WIDTH, MLSTM_HEADS, MLSTM_HEADS,
         CONV_WIDTH, CONV_WIDTH, CONV_WIDTH,
         N_BRANCH * D_MODEL)

LANES = 128
SUBLANES = 8
VMEM_LIMIT_BYTES = 56 * 1024 * 1024

OFF_AQ = 0
OFF_AK = OFF_AQ + ATTN_WIDTH
OFF_AV = OFF_AK + KV_WIDTH
OFF_M = OFF_AV + KV_WIDTH
OFF_C = OFF_M + 4 * MLSTM_WIDTH
OFF_G = OFF_C + 3 * CONV_WIDTH
OFF_IF = OFF_G + N_BRANCH * D_MODEL
P_PACKED = OFF_IF + 2 * LANES
GATE_PART = D_MODEL

Q_HEAD_ORDER = (0, 4, 1, 5, 2, 6, 3, 7)

BF16 = jnp.bfloat16
F32 = jnp.float32


def _dot(a, b):
    return jnp.dot(a, b, preferred_element_type=F32)


def _dot_nt(a, b):
    return lax.dot_general(a, b, (((1,), (1,)), ((), ())), preferred_element_type=F32)


def _dot_tn(a, b):
    return lax.dot_general(a, b, (((0,), (0,)), ((), ())), preferred_element_type=F32)


def _layer_norm_rows(y, w, b):
    mu = jnp.mean(y, axis=-1, keepdims=True)
    yc = y - mu
    var = jnp.mean(yc * yc, axis=-1, keepdims=True)
    return yc * lax.rsqrt(var + LN_EPS) * w + b


def _sigmoid(x):
    return 0.5 * jnp.tanh(0.5 * x) + 0.5


def _log_sigmoid(x):
    return jnp.minimum(x, 0.0) - jnp.log1p(jnp.exp(-jnp.abs(x)))


def _ffn_kernel(x_ref, wgu_ref, wd_ref, lnw_ref, lnb_ref, o_ref, *, chunks):
    d_ff = wd_ref.shape[0]
    x = x_ref[...]
    xb = x.astype(BF16)
    acc = None
    for c0, c1 in chunks:
        g = _dot(xb, wgu_ref[:, c0:c1])
        u = _dot(xb, wgu_ref[:, d_ff + c0:d_ff + c1])
        a = (g * _sigmoid(g) * u).astype(BF16)
        part = _dot(a, wd_ref[c0:c1, :])
        acc = part if acc is None else acc + part
    y = ALPHA * x + 0.5 * acc
    o_ref[...] = _layer_norm_rows(y, lnw_ref[...], lnb_ref[...])


def _ffn_tiles(n_tokens, d_ff):
    tm = 512 if n_tokens % 512 == 0 else 128
    step = 2 * LANES
    chunks = tuple((c, min(c + step, d_ff)) for c in range(0, d_ff, step))
    return tm, chunks


def _ffn_layer(x2d, layer, w_gu, w_down, ln_w, ln_b, ln_idx):
    n, d = x2d.shape
    d_ff = w_down.shape[1]
    tm, chunks = _ffn_tiles(n, d_ff)

    def const(shape, idx):
        return pl.BlockSpec(shape, lambda i: idx, pipeline_mode=pl.Buffered(1))

    return pl.pallas_call(
        functools.partial(_ffn_kernel, chunks=chunks),
        out_shape=jax.ShapeDtypeStruct((n, d), F32),
        grid=(n // tm,),
        in_specs=[
            pl.BlockSpec((tm, d), lambda i: (i, 0)),
            const((None, d, 2 * d_ff), (layer, 0, 0)),
            const((None, d_ff, d), (layer, 0, 0)),
            const((None, None, 1, d), (layer, ln_idx, 0, 0)),
            const((None, None, 1, d), (layer, ln_idx, 0, 0)),
        ],
        out_specs=pl.BlockSpec((tm, d), lambda i: (i, 0)),
        compiler_params=pltpu.CompilerParams(
            dimension_semantics=("arbitrary",),
            vmem_limit_bytes=VMEM_LIMIT_BYTES),
        name="swiglu_ln",
    )(x2d, w_gu, w_down, ln_w, ln_b)


def _rope_tables(pos_row, freq, expand):
    ang = freq * pos_row.astype(F32)
    pieces = []
    for t in (jnp.cos(ang), jnp.sin(ang)):
        hi = t.astype(BF16)
        r1 = t - hi.astype(F32)
        mid = r1.astype(BF16)
        lo = (r1 - mid.astype(F32)).astype(BF16)
        pieces += [hi, mid, lo]
    pieces += [jnp.ones((SUBLANES, LANES), BF16), jnp.zeros((SUBLANES, LANES), BF16)]
    cs = _dot_tn(jnp.concatenate(pieces, axis=0), expand)
    return cs[:, 0:LANES], cs[:, LANES:]


def _rope(t, cos_f, sin_f):
    half = ROT_DIM // 2
    lane = lax.broadcasted_iota(jnp.int32, (1, LANES), 1)
    up = pltpu.roll(t, LANES - half, axis=1)
    down = pltpu.roll(t, half, axis=1)
    return t * cos_f + jnp.where((lane & (HEAD_DIM - 1)) < half, up, down) * sin_f


def _split_kv(prev, cur):
    lane = lax.broadcasted_iota(jnp.int32, (1, LANES), 1)
    half0 = lane < HEAD_DIM
    zero = jnp.zeros((), BF16)
    both = jnp.concatenate([prev, cur], axis=0)
    return jnp.concatenate([jnp.where(half0, both, zero),
                            jnp.where(half0, zero, both)], axis=0)


def _attention_scores(q_all, k_cur, k_prev):
    return _dot_nt(q_all, _split_kv(k_prev, k_cur))


def _attention_softmax(s_all, thr, sinks_ref):
    L = ATTN_BLOCK
    lane = lax.broadcasted_iota(jnp.int32, (1, LANES), 1)
    half0 = lane < HEAD_DIM
    qi = lax.broadcasted_iota(jnp.int32, (L, 2 * L), 0)
    kj = lax.broadcasted_iota(jnp.int32, (L, 2 * L), 1)
    rel = qi + L - kj
    mask = (rel >= 0) & (rel < WINDOW) & (kj >= thr)
    ps, scales = [], []
    for c in range(N_Q_HEADS // 2):
        pg, inv = [], []
        for g in range(N_KV_HEADS):
            sink = sinks_ref[Q_HEAD_ORDER[2 * c + g]] * LOG2E
            sg = jnp.where(mask, s_all[c * L:(c + 1) * L, g * 2 * L:(g + 1) * 2 * L], NEG_INF)
            m = jnp.maximum(jnp.max(sg, axis=-1, keepdims=True), sink)
            p = jnp.exp2(sg - m)
            den = jnp.sum(p, axis=-1, keepdims=True) + jnp.exp2(sink - m)
            pg.append(p.astype(BF16))
            inv.append(1.0 / den)
        ps.append(jnp.concatenate(pg, axis=1))
        scales.append(jnp.where(half0, inv[0], inv[1]))
    return jnp.concatenate(ps, axis=0), jnp.concatenate(scales, axis=0)


def _mlstm_gates(gif, row0):
    L = MLSTM_CHUNK
    rows = slice(row0, row0 + L)
    lane = lax.broadcasted_iota(jnp.int32, (1, LANES), 1)
    gi = gif[rows, 0:LANES]
    gf = gif[rows, LANES:2 * LANES]
    lf = jnp.where(lane < 2 * MLSTM_HEADS, _log_sigmoid(gf), 0.0)
    r_i = lax.broadcasted_iota(jnp.int32, (L, L), 0)
    c_i = lax.broadcasted_iota(jnp.int32, (L, L), 1)
    tril = jnp.where(r_i >= c_i, 1.0, 0.0).astype(BF16)
    hi = lf.astype(BF16)
    r1 = lf - hi.astype(F32)
    mid = r1.astype(BF16)
    lo = (r1 - mid.astype(F32)).astype(BF16)
    bsum = _dot(tril, jnp.concatenate([hi, mid, lo], axis=1))
    bsum = bsum[:, 0:LANES] + bsum[:, LANES:2 * LANES] + bsum[:, 2 * LANES:]
    t_col = jnp.where(lane < MLSTM_HEADS, gi - bsum, bsum)
    return t_col, t_col.T


def _mlstm_stage1(mqk, t_col, t_row, s_ref, m_ref, row0):
    L = MLSTM_CHUNK
    dh = MLSTM_HEAD_DIM
    rows = slice(row0, row0 + L)
    r_i = lax.broadcasted_iota(jnp.int32, (L, L), 0)
    c_i = lax.broadcasted_iota(jnp.int32, (L, L), 1)
    causal = r_i >= c_i
    scale = dh ** -0.5
    heads = []
    for h in range(MLSTM_HEADS):
        q = mqk[rows, h * dh:(h + 1) * dh].astype(BF16)
        k = mqk[rows, MLSTM_WIDTH + h * dh:MLSTM_WIDTH + (h + 1) * dh] * scale
        state = s_ref[h]
        qk = _dot_nt(q, k.astype(BF16))
        from_state = _dot(q, state.astype(BF16))
        heads.append(dict(k=k, state=state, qk=qk, from_state=from_state))
    for h, hd in enumerate(heads):
        b_col = t_col[:, MLSTM_HEADS + h:MLSTM_HEADS + h + 1]
        d_col = t_col[:, h:h + 1]
        d_row = t_row[h:h + 1, :]
        m_prev = m_ref[h][0:1, 0:1]
        logw = jnp.where(causal, b_col + d_row, NEG_INF)
        inter = b_col + m_prev
        m_t = jnp.maximum(inter, jnp.max(logw, axis=-1, keepdims=True))
        scores = hd["qk"] * jnp.exp(logw - m_t)
        m_new = m_t[L - 1:L, :]
        b_last = b_col[L - 1:L, :]
        ws = jnp.exp(b_last + d_col - m_new)
        hd.update(m_t=m_t, w_inter=jnp.exp(inter - m_t), scores=scores,
                  row_sum=jnp.sum(scores, axis=-1, keepdims=True),
                  decay=jnp.exp(b_last + m_prev - m_new), m_new=m_new,
                  kw=(hd["k"] * ws).astype(BF16))
    return heads


def _mlstm_stage2(mvo, heads, s_ref, m_ref, nw_ref, row0):
    L = MLSTM_CHUNK
    dh = MLSTM_HEAD_DIM
    rows = slice(row0, row0 + L)
    lane = lax.broadcasted_iota(jnp.int32, (1, LANES), 1)
    ones_tile = jnp.broadcast_to(jnp.where(lane == 0, 1.0, 0.0).astype(BF16), (L, LANES))
    outs = []
    for h, hd in enumerate(heads):
        v = mvo[rows, h * dh:(h + 1) * dh].astype(BF16)
        hd["intra"] = _dot(hd["scores"].astype(BF16), v)
        v_ext = jnp.concatenate([v, ones_tile], axis=1)
        s_ref[h] = hd["decay"] * hd["state"] + _dot_tn(hd["kw"], v_ext)
        m_ref[h] = jnp.broadcast_to(hd["m_new"], (SUBLANES, LANES))
    for h, hd in enumerate(heads):
        og = mvo[rows, MLSTM_WIDTH + h * dh:MLSTM_WIDTH + (h + 1) * dh]
        num = hd["w_inter"] * hd["from_state"][:, 0:dh] + hd["intra"]
        den = hd["w_inter"] * hd["from_state"][:, dh:dh + 1] + hd["row_sum"]
        hh = num / jnp.maximum(jnp.abs(den), jnp.exp(-hd["m_t"]))
        mu = jnp.mean(hh, axis=-1, keepdims=True)
        hc = hh - mu
        hn = hc * lax.rsqrt(jnp.mean(hc * hc, axis=-1, keepdims=True) + LN_EPS)
        outs.append(hn * nw_ref[:, h * dh:(h + 1) * dh] * _sigmoid(og))
    return outs


class _SubTile:
    def __init__(self, sub, tsub, s_idx, refs):
        self.sub, self.tsub, self.s_idx, self.r = sub, tsub, s_idx, refs
        self.r0 = sub * tsub
        self.nblk = tsub // ATTN_BLOCK
        self.x = refs["x"][0, self.r0:self.r0 + tsub, :]
        self.xb = self.x.astype(BF16)
        self.gates = [[None] * (D_MODEL // GATE_PART) for _ in range(N_BRANCH)]
        self.merged = None

    def proj(self, lo, hi):
        return _dot(self.xb, self.r["win"][:, lo:hi]) + self.r["bin"][:, lo:hi]

    def ropes(self):
        r, n = self.r, self.nblk
        self.rope = [_rope_tables(r["pos"][0, self.sub * n + j:self.sub * n + j + 1, :],
                                  r["freq"][...], r["expand"][...]) for j in range(n)]

    def proj_attn(self):
        self.qkv = self.proj(OFF_AQ, OFF_M)

    def proj_gates(self):
        gif = self.proj(OFF_IF, P_PACKED)
        self.mgates = [_mlstm_gates(gif, j * MLSTM_CHUNK) for j in range(self.nblk)]

    def proj_mlstm_qk(self):
        self.mqk = self.proj(OFF_M, OFF_M + 2 * MLSTM_WIDTH)

    def proj_mlstm_vo(self):
        self.mvo = self.proj(OFF_M + 2 * MLSTM_WIDTH, OFF_C)

    def conv(self):
        r, r0, tsub = self.r, self.r0, self.tsub
        cproj = self.proj(OFF_C, OFF_G)
        u = cproj[:, 2 * CONV_WIDTH:] * cproj[:, 0:CONV_WIDTH]
        r["conv"][SUBLANES + r0:SUBLANES + r0 + tsub, :] = u
        u1 = r["conv"][pl.ds(SUBLANES + r0 - 1, tsub), :]
        u2 = r["conv"][pl.ds(SUBLANES + r0 - 2, tsub), :]
        yc = r["cw"][0:1, :] * u2 + r["cw"][1:2, :] * u1 + r["cw"][2:3, :] * u
        r["y"][r0:r0 + tsub, 2 * BRANCH_WIDTH:] = (cproj[:, CONV_WIDTH:2 * CONV_WIDTH] * yc).astype(BF16)

    def gate(self, b, part):
        lo = OFF_G + b * D_MODEL + part * GATE_PART
        self.gates[b][part] = jnp.tanh(self.proj(lo, lo + GATE_PART))

    def stage1(self, j, kv):
        L = ATTN_BLOCK
        rows = slice(j * L, (j + 1) * L)
        cos_f, sin_f = self.rope[j]
        q_all = jnp.concatenate(
            [(_rope(self.qkv[rows, c * LANES:(c + 1) * LANES], cos_f, sin_f)
              * (HEAD_DIM ** -0.5 * LOG2E)).astype(BF16) for c in range(N_Q_HEADS // 2)], axis=0)
        self.k_cur = _rope(self.qkv[rows, OFF_AK:OFF_AK + LANES], cos_f, sin_f).astype(BF16)
        self.v_cur = self.qkv[rows, OFF_AV:OFF_AV + LANES].astype(BF16)
        thr = jnp.where(self.s_idx == 0, L, 0) if (self.sub == 0 and j == 0) else 0
        s_all = _attention_scores(q_all, self.k_cur, kv[0])
        self.heads = _mlstm_stage1(self.mqk, self.mgates[j][0], self.mgates[j][1],
                                   self.r["s"], self.r["m"], j * L)
        self.p_all, self.o_scale = _attention_softmax(s_all, thr, self.r["sinks"])

    def stage2(self, j, kv):
        L = ATTN_BLOCK
        r = self.r
        yrows = slice(self.r0 + j * L, self.r0 + (j + 1) * L)
        o_all = _dot(self.p_all, _split_kv(kv[1], self.v_cur)) * self.o_scale
        for c in range(N_Q_HEADS // 2):
            r["y"][yrows, c * LANES:(c + 1) * LANES] = o_all[c * L:(c + 1) * L, :].astype(BF16)
        outs = _mlstm_stage2(self.mvo, self.heads, r["s"], r["m"], r["nw"], j * L)
        for h, o in enumerate(outs):
            r["y"][yrows, BRANCH_WIDTH + h * MLSTM_HEAD_DIM:
                   BRANCH_WIDTH + (h + 1) * MLSTM_HEAD_DIM] = o.astype(BF16)
        return self.k_cur, self.v_cur

    def branch(self, b):
        r, r0, tsub = self.r, self.r0, self.tsub
        p = _dot(r["y"][r0:r0 + tsub, b * BRANCH_WIDTH:(b + 1) * BRANCH_WIDTH], r["wb"][b])
        term = jnp.concatenate(self.gates[b], axis=1) * p + p
        self.merged = term if self.merged is None else self.merged + term

    def finish(self):
        r = self.r
        mix = _dot(self.merged.astype(BF16), r["wo"][...])
        y = ALPHA * self.x + mix
        r["o"][0, self.r0:self.r0 + self.tsub, :] = _layer_norm_rows(y, r["lnw"][...], r["lnb"][...])


def _mixer_kernel(sinks_ref, x_ref, pos_ref, freq_ref, expand_ref, win_ref, bin_ref, nw_ref, cw_ref,
                  wb_ref, wo_ref, lnw_ref, lnb_ref, o_ref,
                  kprev_ref, vprev_ref, s_ref, m_ref, conv_ref, y_ref, *, tm, tsub):
    s_idx = pl.program_id(1)

    @pl.when(s_idx == 0)
    def _reset():
        kprev_ref[...] = jnp.zeros_like(kprev_ref)
        vprev_ref[...] = jnp.zeros_like(vprev_ref)
        s_ref[...] = jnp.zeros_like(s_ref)
        m_ref[...] = jnp.zeros_like(m_ref)
        conv_ref[0:SUBLANES, :] = jnp.zeros((SUBLANES, CONV_WIDTH), F32)

    refs = dict(sinks=sinks_ref, x=x_ref, pos=pos_ref, freq=freq_ref, expand=expand_ref, win=win_ref,
                bin=bin_ref, nw=nw_ref, cw=cw_ref, wb=wb_ref, wo=wo_ref, lnw=lnw_ref, lnb=lnb_ref,
                o=o_ref, s=s_ref, m=m_ref, conv=conv_ref, y=y_ref)
    nsub = tm // tsub
    tiles = [_SubTile(t, tsub, s_idx, refs) for t in range(nsub)]

    def head_steps(t):
        return [t.ropes, t.proj_attn, t.proj_gates, t.proj_mlstm_qk, t.proj_mlstm_vo]

    def tail_steps(t):
        return [lambda: t.branch(2), lambda: t.branch(0), lambda: t.branch(1), t.finish]

    for step in head_steps(tiles[0]):
        step()
    kv = (kprev_ref[...], vprev_ref[...])
    carry_tail = []
    for i, t in enumerate(tiles):
        own = [t.conv] + [functools.partial(t.gate, b, p) for b in range(N_BRANCH)
                          for p in range(D_MODEL // GATE_PART)]
        nxt = head_steps(tiles[i + 1]) if i + 1 < nsub else []
        fillers = carry_tail + own + nxt
        slots = 2 * t.nblk
        for slot in range(slots):
            j = slot // 2
            if slot % 2 == 0:
                t.stage1(j, kv)
            else:
                kv = t.stage2(j, kv)
            take = -(-len(fillers) // (slots - slot))
            for step in fillers[:take]:
                step()
            fillers = fillers[take:]
        carry_tail = tail_steps(t)
    for step in carry_tail:
        step()
    kprev_ref[...], vprev_ref[...] = kv
    conv_ref[0:SUBLANES, :] = conv_ref[tm:tm + SUBLANES, :]


ROPE_K = 8 * SUBLANES


def _rope_constants():
    half = ROT_DIM // 2
    inv_freq = ROPE_THETA ** (-np.arange(0, ROT_DIM, 2, dtype=np.float64) / ROT_DIM)
    freq = np.repeat(inv_freq[:, None], LANES, axis=1).astype(np.float32)
    l64 = np.arange(LANES) % HEAD_DIM
    expand = np.zeros((ROPE_K, 2 * LANES), np.float32)
    for f in range(half):
        cos_lanes = np.where((l64 < ROT_DIM) & (l64 % half == f), 1.0, 0.0)
        sin_lanes = np.where(l64 == f, -1.0, 0.0) + np.where(l64 == f + half, 1.0, 0.0)
        for piece in range(3):
            expand[piece * SUBLANES + f, :LANES] = cos_lanes
            expand[(3 + piece) * SUBLANES + f, LANES:] = sin_lanes
    expand[6 * SUBLANES, :LANES] = np.where(l64 >= ROT_DIM, 1.0, 0.0)
    return jnp.asarray(freq), jnp.asarray(expand, dtype=BF16)


def _pack_in_proj(w):
    offs = np.concatenate([[0], np.cumsum(SIZES)]).tolist()
    seg = [w[..., offs[i]:offs[i + 1]] for i in range(len(SIZES))]
    (aq, ak, av, mq, mk, mv, mo, mi, mf, cx, cb, cc, gates) = seg
    aq_heads = [aq[..., h * HEAD_DIM:(h + 1) * HEAD_DIM] for h in Q_HEAD_ORDER]
    zi = jnp.zeros(w.shape[:-1] + (LANES - MLSTM_HEADS,), w.dtype)
    zf = jnp.zeros(w.shape[:-1] + (LANES - 2 * MLSTM_HEADS,), w.dtype)
    return jnp.concatenate(aq_heads + [ak, av, mq, mk, mv, mo, cx, cb, cc, gates,
                                       mi, zi, mf, mf, zf], axis=-1)


def _pack_kernel(w_ref, o_ref):
    offs = np.concatenate([[0], np.cumsum(SIZES)]).tolist()
    rows = w_ref.shape[0]
    lane = lax.broadcasted_iota(jnp.int32, (1, LANES), 1)
    for c in range(N_Q_HEADS // 2):
        pair = [w_ref[:, h * HEAD_DIM:(h + 1) * HEAD_DIM] for h in Q_HEAD_ORDER[2 * c:2 * c + 2]]
        o_ref[:, c * LANES:(c + 1) * LANES] = jnp.concatenate(pair, axis=1).astype(BF16)
    o_ref[:, OFF_AK:OFF_C] = w_ref[:, offs[1]:offs[7]].astype(BF16)
    o_ref[:, OFF_C:OFF_G] = w_ref[:, offs[9]:offs[12]].astype(BF16)
    o_ref[:, OFF_G:OFF_IF] = (0.5 * w_ref[:, offs[12]:offs[13]]).astype(BF16)
    t = w_ref[:, offs[7]:offs[7] + LANES]
    o_ref[:, OFF_IF:OFF_IF + LANES] = jnp.where(lane < MLSTM_HEADS, t, 0.0).astype(BF16)
    f_lo = pltpu.roll(t, LANES - MLSTM_HEADS, axis=1)
    f_tile = jnp.where(lane < MLSTM_HEADS, f_lo, jnp.where(lane < 2 * MLSTM_HEADS, t, 0.0))
    o_ref[:, OFF_IF + LANES:P_PACKED] = f_tile.astype(BF16)


def _pack_in_proj_weights(w_in):
    depth, d, p_in = w_in.shape
    rows = 256
    return pl.pallas_call(
        _pack_kernel,
        out_shape=jax.ShapeDtypeStruct((depth, d, P_PACKED), BF16),
        grid=(depth, d // rows),
        in_specs=[pl.BlockSpec((None, rows, p_in), lambda l, i: (l, i, 0))],
        out_specs=pl.BlockSpec((None, rows, P_PACKED), lambda l, i: (l, i, 0)),
        compiler_params=pltpu.CompilerParams(
            dimension_semantics=("arbitrary", "arbitrary"),
            vmem_limit_bytes=VMEM_LIMIT_BYTES),
        name="pack_in_proj",
    )(w_in)


def _pack_branch(w_branch):
    wb0 = jnp.concatenate([w_branch[:, 0:1, h * HEAD_DIM:(h + 1) * HEAD_DIM, :] for h in Q_HEAD_ORDER],
                          axis=2)
    return jnp.concatenate([wb0, w_branch[:, 1:]], axis=1)


def _mixer_tile(seq):
    if seq % 512 == 0:
        return 512, 256
    if seq % 256 == 0:
        return 256, 256
    return ATTN_BLOCK, ATTN_BLOCK


def _mixer_layer(x, layer, positions, freq, expand, win, bin_, sinks, norm_w, conv_w, wb, wo, ln_w, ln_b):
    bsz, seq, d = x.shape
    tm, tsub = _mixer_tile(seq)
    ns = seq // tm
    nblk = tm // ATTN_BLOCK
    pos = positions.reshape(bsz * ns, nblk, ATTN_BLOCK)

    def const(shape, idx):
        return pl.BlockSpec(shape, lambda b, s, *_: idx, pipeline_mode=pl.Buffered(1))

    grid_spec = pltpu.PrefetchScalarGridSpec(
        num_scalar_prefetch=1,
        grid=(bsz, ns),
        in_specs=[
            pl.BlockSpec((1, tm, d), lambda b, s, *_: (b, s, 0)),
            pl.BlockSpec((1, nblk, ATTN_BLOCK), lambda b, s, *_: (b * ns + s, 0, 0)),
            const((SUBLANES, LANES), (0, 0)),
            const((ROPE_K, 2 * LANES), (0, 0)),
            const((None, d, P_PACKED), (layer, 0, 0)),
            const((None, 1, P_PACKED), (layer, 0, 0)),
            const((None, 1, MLSTM_WIDTH), (layer, 0, 0)),
            const((None, CONV_K, CONV_WIDTH), (layer, 0, 0)),
            const((None, N_BRANCH, BRANCH_WIDTH, d), (layer, 0, 0, 0)),
            const((None, d, d), (layer, 0, 0)),
            const((None, None, 1, d), (layer, 1, 0, 0)),
            const((None, None, 1, d), (layer, 1, 0, 0)),
        ],
        out_specs=pl.BlockSpec((1, tm, d), lambda b, s, *_: (b, s, 0)),
        scratch_shapes=[
            pltpu.VMEM((ATTN_BLOCK, LANES), BF16),
            pltpu.VMEM((ATTN_BLOCK, LANES), BF16),
            pltpu.VMEM((MLSTM_HEADS, MLSTM_HEAD_DIM, 2 * MLSTM_HEAD_DIM), F32),
            pltpu.VMEM((MLSTM_HEADS, SUBLANES, LANES), F32),
            pltpu.VMEM((tm + 2 * SUBLANES, CONV_WIDTH), F32),
            pltpu.VMEM((tm, N_BRANCH * BRANCH_WIDTH), BF16),
        ],
    )
    return pl.pallas_call(
        functools.partial(_mixer_kernel, tm=tm, tsub=tsub),
        out_shape=jax.ShapeDtypeStruct((bsz, seq, d), F32),
        grid_spec=grid_spec,
        compiler_params=pltpu.CompilerParams(
            dimension_semantics=("arbitrary", "arbitrary"),
            vmem_limit_bytes=VMEM_LIMIT_BYTES),
        name="token_mixer",
    )(sinks, x, pos, freq, expand, win, bin_, norm_w, conv_w, wb, wo, ln_w, ln_b)


def kernel(x, positions, w_in, b_in, attn_sinks, mlstm_norm_w, conv_w, w_branch, w_out,
           ffn1_w_gu, ffn1_w_down, ffn2_w_gu, ffn2_w_down, ln_w, ln_b):
    bsz, seq, d = x.shape
    depth = w_in.shape[0]
    ffn_w = ((ffn1_w_gu.astype(BF16), ffn1_w_down.astype(BF16)),
             (ffn2_w_gu.astype(BF16), ffn2_w_down.astype(BF16)))
    win = _pack_in_proj_weights(w_in)
    bin_ = _pack_in_proj(b_in)
    bin_ = bin_.at[:, OFF_G:OFF_IF].multiply(0.5)[:, None, :]
    wb = _pack_branch(w_branch.astype(BF16))
    wo = (0.5 * w_out).astype(BF16)
    norm_w = mlstm_norm_w[:, None, :]
    ln_w4 = ln_w[:, :, None, :]
    ln_b4 = ln_b[:, :, None, :]
    freq, expand = _rope_constants()
    for l in range(depth):
        x = _ffn_layer(x.reshape(bsz * seq, d), l, ffn_w[0][0], ffn_w[0][1], ln_w4, ln_b4, 0
                       ).reshape(bsz, seq, d)
        x = _mixer_layer(x, l, positions, freq, expand, win, bin_, attn_sinks[l], norm_w, conv_w, wb, wo,
                         ln_w4, ln_b4)
        x = _ffn_layer(x.reshape(bsz * seq, d), l, ffn_w[1][0], ffn_w[1][1], ln_w4, ln_b4, 2
                       ).reshape(bsz, seq, d)
    return x
```

```python
import functools
import math

import numpy as np
import jax
import jax.numpy as jnp
from jax import lax
from jax.experimental import pallas as pl
from jax.experimental.pallas import tpu as pltpu

D_MODEL = 1024
DEPTH = 2

HEAD_DIM = 64
N_Q_HEADS = 8
N_KV_HEADS = 2
GROUP = N_Q_HEADS // N_KV_HEADS
WINDOW = 128
ATTN_BLOCK = 128
ROPE_THETA = 500000.0
ROT_DIM = HEAD_DIM // 4
ATTN_WIDTH = N_Q_HEADS * HEAD_DIM
KV_WIDTH = N_KV_HEADS * HEAD_DIM

MLSTM_HEADS = 4
MLSTM_HEAD_DIM = D_MODEL // 8
MLSTM_WIDTH = MLSTM_HEADS * MLSTM_HEAD_DIM
MLSTM_CHUNK = 128

CONV_WIDTH = D_MODEL // 2
CONV_K = 3

N_BRANCH = 3
BRANCH_WIDTH = 512
D_FF = 2816
LN_EPS = 1e-5
ALPHA = (2.0 * DEPTH) ** 0.25
NEG_INF = -1e30

SIZES = (ATTN_WIDTH, KV_WIDTH, KV_WIDTH,
         MLSTM_WIDTH, MLSTM_WIDTH, MLSTM_WIDTH, MLSTM_WIDTH, MLSTM_HEADS, MLSTM_HEADS,
         CONV_WIDTH, CONV_WIDTH, CONV_WIDTH,
         N_BRANCH * D_MODEL)

LANES = 128
SUBLANES = 8
VMEM_LIMIT_BYTES = 56 * 1024 * 1024

OFF_AQ = 0
OFF_AK = OFF_AQ + ATTN_WIDTH
OFF_AV = OFF_AK + KV_WIDTH
OFF_M = OFF_AV + KV_WIDTH
OFF_C = OFF_M + 4 * MLSTM_WIDTH
OFF_G = OFF_C + 3 * CONV_WIDTH
OFF_IF = OFF_G + N_BRANCH * D_MODEL
P_PACKED = OFF_IF + 2 * LANES
GATE_PART = D_MODEL

Q_HEAD_ORDER = (0, 4, 1, 5, 2, 6, 3, 7)

BF16 = jnp.bfloat16
F32 = jnp.float32


def _dot(a, b):
    return jnp.dot(a, b, preferred_element_type=F32)


def _dot_nt(a, b):
    return lax.dot_general(a, b, (((1,), (1,)), ((), ())), preferred_element_type=F32)


def _dot_tn(a, b):
    return lax.dot_general(a, b, (((0,), (0,)), ((), ())), preferred_element_type=F32)


def _layer_norm_rows(y, w, b):
    mu = jnp.mean(y, axis=-1, keepdims=True)
    yc = y - mu
    var = jnp.mean(yc * yc, axis=-1, keepdims=True)
    return yc * lax.rsqrt(var + LN_EPS) * w + b


def _sigmoid(x):
    return 0.5 * jnp.tanh(0.5 * x) + 0.5


def _log_sigmoid(x):
    return jnp.minimum(x, 0.0) - jnp.log1p(jnp.exp(-jnp.abs(x)))


def _ffn_kernel(x_ref, wgu_ref, wd_ref, lnw_ref, lnb_ref, o_ref, *, chunks):
    d_ff = wd_ref.shape[0]
    x = x_ref[...]
    xb = x.astype(BF16)
    acc = None
    for c0, c1 in chunks:
        g = _dot(xb, wgu_ref[:, c0:c1])
        u = _dot(xb, wgu_ref[:, d_ff + c0:d_ff + c1])
        a = (g * _sigmoid(g) * u).astype(BF16)
        part = _dot(a, wd_ref[c0:c1, :])
        acc = part if acc is None else acc + part
    y = ALPHA * x + 0.5 * acc
    o_ref[...] = _layer_norm_rows(y, lnw_ref[...], lnb_ref[...])


def _ffn_tiles(n_tokens, d_ff):
    tm = 512 if n_tokens % 512 == 0 else 128
    step = 2 * LANES
    chunks = tuple((c, min(c + step, d_ff)) for c in range(0, d_ff, step))
    return tm, chunks


def _ffn_layer(x2d, layer, w_gu, w_down, ln_w, ln_b, ln_idx):
    n, d = x2d.shape
    d_ff = w_down.shape[1]
    tm, chunks = _ffn_tiles(n, d_ff)

    def const(shape, idx):
        return pl.BlockSpec(shape, lambda i: idx, pipeline_mode=pl.Buffered(1))

    return pl.pallas_call(
        functools.partial(_ffn_kernel, chunks=chunks),
        out_shape=jax.ShapeDtypeStruct((n, d), F32),
        grid=(n // tm,),
        in_specs=[
            pl.BlockSpec((tm, d), lambda i: (i, 0)),
            const((None, d, 2 * d_ff), (layer, 0, 0)),
            const((None, d_ff, d), (layer, 0, 0)),
            const((None, None, 1, d), (layer, ln_idx, 0, 0)),
            const((None, None, 1, d), (layer, ln_idx, 0, 0)),
        ],
        out_specs=pl.BlockSpec((tm, d), lambda i: (i, 0)),
        compiler_params=pltpu.CompilerParams(
            dimension_semantics=("arbitrary",),
            vmem_limit_bytes=VMEM_LIMIT_BYTES),
        name="swiglu_ln",
    )(x2d, w_gu, w_down, ln_w, ln_b)


def _rope_tables(pos_row, freq, expand):
    ang = freq * pos_row.astype(F32)
    pieces = []
    for t in (jnp.cos(ang), jnp.sin(ang)):
        hi = t.astype(BF16)
        r1 = t - hi.astype(F32)
        mid = r1.astype(BF16)
        lo = (r1 - mid.astype(F32)).astype(BF16)
        pieces += [hi, mid, lo]
    pieces += [jnp.ones((SUBLANES, LANES), BF16), jnp.zeros((SUBLANES, LANES), BF16)]
    cs = _dot_tn(jnp.concatenate(pieces, axis=0), expand)
    return cs[:, 0:LANES], cs[:, LANES:]


def _rope(t, cos_f, sin_f):
    half = ROT_DIM // 2
    lane = lax.broadcasted_iota(jnp.int32, (1, LANES), 1)
    up = pltpu.roll(t, LANES - half, axis=1)
    down = pltpu.roll(t, half, axis=1)
    return t * cos_f + jnp.where((lane & (HEAD_DIM - 1)) < half, up, down) * sin_f


def _split_kv(prev, cur):
    lane = lax.broadcasted_iota(jnp.int32, (1, LANES), 1)
    half0 = lane < HEAD_DIM
    zero = jnp.zeros((), BF16)
    both = jnp.concatenate([prev, cur], axis=0)
    return jnp.concatenate([jnp.where(half0, both, zero),
                            jnp.where(half0, zero, both)], axis=0)


def _attention_scores(q_all, k_cur, k_prev):
    return _dot_nt(q_all, _split_kv(k_prev, k_cur))


def _attention_softmax(s_all, thr, sinks_ref):
    L = ATTN_BLOCK
    lane = lax.broadcasted_iota(jnp.int32, (1, LANES), 1)
    half0 = lane < HEAD_DIM
    qi = lax.broadcasted_iota(jnp.int32, (L, 2 * L), 0)
    kj = lax.broadcasted_iota(jnp.int32, (L, 2 * L), 1)
    rel = qi + L - kj
    mask = (rel >= 0) & (rel < WINDOW) & (kj >= thr)
    ps, scales = [], []
    for c in range(N_Q_HEADS // 2):
        pg, inv = [], []
        for g in range(N_KV_HEADS):
            sink = sinks_ref[Q_HEAD_ORDER[2 * c + g]]
            sg = jnp.where(mask, s_all[c * L:(c + 1) * L, g * 2 * L:(g + 1) * 2 * L], NEG_INF)
            m = jnp.maximum(jnp.max(sg, axis=-1, keepdims=True), sink)
            p = jnp.exp(sg - m)
            den = jnp.sum(p, axis=-1, keepdims=True) + jnp.exp(sink - m)
            pg.append(p.astype(BF16))
            inv.append(1.0 / den)
        ps.append(jnp.concatenate(pg, axis=1))
        scales.append(jnp.where(half0, inv[0], inv[1]))
    return jnp.concatenate(ps, axis=0), jnp.concatenate(scales, axis=0)


def _mlstm_gates(gif, row0):
    L = MLSTM_CHUNK
    rows = slice(row0, row0 + L)
    lane = lax.broadcasted_iota(jnp.int32, (1, LANES), 1)
    gi = gif[rows, 0:LANES]
    gf = gif[rows, LANES:2 * LANES]
    lf = jnp.where(lane < 2 * MLSTM_HEADS, _log_sigmoid(gf), 0.0)
    r_i = lax.broadcasted_iota(jnp.int32, (L, L), 0)
    c_i = lax.broadcasted_iota(jnp.int32, (L, L), 1)
    tril = jnp.where(r_i >= c_i, 1.0, 0.0).astype(BF16)
    hi = lf.astype(BF16)
    r1 = lf - hi.astype(F32)
    mid = r1.astype(BF16)
    lo = (r1 - mid.astype(F32)).astype(BF16)
    bsum = _dot(tril, jnp.concatenate([hi, mid, lo], axis=1))
    bsum = bsum[:, 0:LANES] + bsum[:, LANES:2 * LANES] + bsum[:, 2 * LANES:]
    t_col = jnp.where(lane < MLSTM_HEADS, gi - bsum, bsum)
    return t_col, t_col.T


def _mlstm_stage1(mqk, t_col, t_row, s_ref, m_ref, row0):
    L = MLSTM_CHUNK
    dh = MLSTM_HEAD_DIM
    rows = slice(row0, row0 + L)
    r_i = lax.broadcasted_iota(jnp.int32, (L, L), 0)
    c_i = lax.broadcasted_iota(jnp.int32, (L, L), 1)
    causal = r_i >= c_i
    scale = dh ** -0.5
    heads = []
    for h in range(MLSTM_HEADS):
        q = mqk[rows, h * dh:(h + 1) * dh].astype(BF16)
        k = mqk[rows, MLSTM_WIDTH + h * dh:MLSTM_WIDTH + (h + 1) * dh] * scale
        state = s_ref[h]
        qk = _dot_nt(q, k.astype(BF16))
        from_state = _dot(q, state.astype(BF16))
        heads.append(dict(k=k, state=state, qk=qk, from_state=from_state))
    for h, hd in enumerate(heads):
        b_col = t_col[:, MLSTM_HEADS + h:MLSTM_HEADS + h + 1]
        d_col = t_col[:, h:h + 1]
        d_row = t_row[h:h + 1, :]
        m_prev = m_ref[h][0:1, 0:1]
        logw = jnp.where(causal, b_col + d_row, NEG_INF)
        inter = b_col + m_prev
        m_t = jnp.maximum(inter, jnp.max(logw, axis=-1, keepdims=True))
        scores = hd["qk"] * jnp.exp(logw - m_t)
        m_new = m_t[L - 1:L, :]
        b_last = b_col[L - 1:L, :]
        ws = jnp.exp(b_last + d_col - m_new)
        hd.update(m_t=m_t, w_inter=jnp.exp(inter - m_t), scores=scores,
                  row_sum=jnp.sum(scores, axis=-1, keepdims=True),
                  decay=jnp.exp(b_last + m_prev - m_new), m_new=m_new,
                  kw=(hd["k"] * ws).astype(BF16))
    return heads


def _mlstm_stage2(mvo, heads, s_ref, m_ref, nw_ref, row0):
    L = MLSTM_CHUNK
    dh = MLSTM_HEAD_DIM
    rows = slice(row0, row0 + L)
    lane = lax.broadcasted_iota(jnp.int32, (1, LANES), 1)
    ones_tile = jnp.broadcast_to(jnp.where(lane == 0, 1.0, 0.0).astype(BF16), (L, LANES))
    outs = []
    for h, hd in enumerate(heads):
        v = mvo[rows, h * dh:(h + 1) * dh].astype(BF16)
        hd["intra"] = _dot(hd["scores"].astype(BF16), v)
        v_ext = jnp.concatenate([v, ones_tile], axis=1)
        s_ref[h] = hd["decay"] * hd["state"] + _dot_tn(hd["kw"], v_ext)
        m_ref[h] = jnp.broadcast_to(hd["m_new"], (SUBLANES, LANES))
    for h, hd in enumerate(heads):
        og = mvo[rows, MLSTM_WIDTH + h * dh:MLSTM_WIDTH + (h + 1) * dh]
        num = hd["w_inter"] * hd["from_state"][:, 0:dh] + hd["intra"]
        den = hd["w_inter"] * hd["from_state"][:, dh:dh + 1] + hd["row_sum"]
        hh = num / jnp.maximum(jnp.abs(den), jnp.exp(-hd["m_t"]))
        mu = jnp.mean(hh, axis=-1, keepdims=True)
        hc = hh - mu
        hn = hc * lax.rsqrt(jnp.mean(hc * hc, axis=-1, keepdims=True) + LN_EPS)
        outs.append(hn * nw_ref[:, h * dh:(h + 1) * dh] * _sigmoid(og))
    return outs


class _SubTile:
    def __init__(self, sub, tsub, s_idx, refs):
        self.sub, self.tsub, self.s_idx, self.r = sub, tsub, s_idx, refs
        self.r0 = sub * tsub
        self.nblk = tsub // ATTN_BLOCK
        self.x = refs["x"][0, self.r0:self.r0 + tsub, :]
        self.xb = self.x.astype(BF16)
        self.gates = [[None] * (D_MODEL // GATE_PART) for _ in range(N_BRANCH)]
        self.merged = None

    def proj(self, lo, hi):
        return _dot(self.xb, self.r["win"][:, lo:hi]) + self.r["bin"][:, lo:hi]

    def ropes(self):
        r, n = self.r, self.nblk
        self.rope = [_rope_tables(r["pos"][0, self.sub * n + j:self.sub * n + j + 1, :],
                                  r["freq"][...], r["expand"][...]) for j in range(n)]

    def proj_attn(self):
        self.qkv = self.proj(OFF_AQ, OFF_M)

    def proj_gates(self):
        gif = self.proj(OFF_IF, P_PACKED)
        self.mgates = [_mlstm_gates(gif, j * MLSTM_CHUNK) for j in range(self.nblk)]

    def proj_mlstm_qk(self):
        self.mqk = self.proj(OFF_M, OFF_M + 2 * MLSTM_WIDTH)

    def proj_mlstm_vo(self):
        self.mvo = self.proj(OFF_M + 2 * MLSTM_WIDTH, OFF_C)

    def conv(self):
        r, r0, tsub = self.r, self.r0, self.tsub
        cproj = self.proj(OFF_C, OFF_G)
        u = cproj[:, 2 * CONV_WIDTH:] * cproj[:, 0:CONV_WIDTH]
        r["conv"][SUBLANES + r0:SUBLANES + r0 + tsub, :] = u
        u1 = r["conv"][pl.ds(SUBLANES + r0 - 1, tsub), :]
        u2 = r["conv"][pl.ds(SUBLANES + r0 - 2, tsub), :]
        yc = r["cw"][0:1, :] * u2 + r["cw"][1:2, :] * u1 + r["cw"][2:3, :] * u
        r["y"][r0:r0 + tsub, 2 * BRANCH_WIDTH:] = (cproj[:, CONV_WIDTH:2 * CONV_WIDTH] * yc).astype(BF16)

    def gate(self, b, part):
        lo = OFF_G + b * D_MODEL + part * GATE_PART
        self.gates[b][part] = _sigmoid(self.proj(lo, lo + GATE_PART))

    def stage1(self, j, kv):
        L = ATTN_BLOCK
        rows = slice(j * L, (j + 1) * L)
        cos_f, sin_f = self.rope[j]
        q_all = jnp.concatenate(
            [(_rope(self.qkv[rows, c * LANES:(c + 1) * LANES], cos_f, sin_f)
              * (HEAD_DIM ** -0.5)).astype(BF16) for c in range(N_Q_HEADS // 2)], axis=0)
        self.k_cur = _rope(self.qkv[rows, OFF_AK:OFF_AK + LANES], cos_f, sin_f).astype(BF16)
        self.v_cur = self.qkv[rows, OFF_AV:OFF_AV + LANES].astype(BF16)
        thr = jnp.where(self.s_idx == 0, L, 0) if (self.sub == 0 and j == 0) else 0
        s_all = _attention_scores(q_all, self.k_cur, kv[0])
        self.heads = _mlstm_stage1(self.mqk, self.mgates[j][0], self.mgates[j][1],
                                   self.r["s"], self.r["m"], j * L)
        self.p_all, self.o_scale = _attention_softmax(s_all, thr, self.r["sinks"])

    def stage2(self, j, kv):
        L = ATTN_BLOCK
        r = self.r
        yrows = slice(self.r0 + j * L, self.r0 + (j + 1) * L)
        o_all = _dot(self.p_all, _split_kv(kv[1], self.v_cur)) * self.o_scale
        for c in range(N_Q_HEADS // 2):
            r["y"][yrows, c * LANES:(c + 1) * LANES] = o_all[c * L:(c + 1) * L, :].astype(BF16)
        outs = _mlstm_stage2(self.mvo, self.heads, r["s"], r["m"], r["nw"], j * L)
        for h, o in enumerate(outs):
            r["y"][yrows, BRANCH_WIDTH + h * MLSTM_HEAD_DIM:
                   BRANCH_WIDTH + (h + 1) * MLSTM_HEAD_DIM] = o.astype(BF16)
        return self.k_cur, self.v_cur

    def branch(self, b):
        r, r0, tsub = self.r, self.r0, self.tsub
        term = jnp.concatenate(self.gates[b], axis=1) * _dot(
            r["y"][r0:r0 + tsub, b * BRANCH_WIDTH:(b + 1) * BRANCH_WIDTH], r["wb"][b])
        self.merged = term if self.merged is None else self.merged + term

    def finish(self):
        r = self.r
        mix = _dot(self.merged.astype(BF16), r["wo"][...])
        y = ALPHA * self.x + mix
        r["o"][0, self.r0:self.r0 + self.tsub, :] = _layer_norm_rows(y, r["lnw"][...], r["lnb"][...])


def _mixer_kernel(sinks_ref, x_ref, pos_ref, freq_ref, expand_ref, win_ref, bin_ref, nw_ref, cw_ref,
                  wb_ref, wo_ref, lnw_ref, lnb_ref, o_ref,
                  kprev_ref, vprev_ref, s_ref, m_ref, conv_ref, y_ref, *, tm, tsub):
    s_idx = pl.program_id(1)

    @pl.when(s_idx == 0)
    def _reset():
        kprev_ref[...] = jnp.zeros_like(kprev_ref)
        vprev_ref[...] = jnp.zeros_like(vprev_ref)
        s_ref[...] = jnp.zeros_like(s_ref)
        m_ref[...] = jnp.zeros_like(m_ref)
        conv_ref[0:SUBLANES, :] = jnp.zeros((SUBLANES, CONV_WIDTH), F32)

    refs = dict(sinks=sinks_ref, x=x_ref, pos=pos_ref, freq=freq_ref, expand=expand_ref, win=win_ref,
                bin=bin_ref, nw=nw_ref, cw=cw_ref, wb=wb_ref, wo=wo_ref, lnw=lnw_ref, lnb=lnb_ref,
                o=o_ref, s=s_ref, m=m_ref, conv=conv_ref, y=y_ref)
    nsub = tm // tsub
    tiles = [_SubTile(t, tsub, s_idx, refs) for t in range(nsub)]

    def head_steps(t):
        return [t.ropes, t.proj_attn, t.proj_gates, t.proj_mlstm_qk, t.proj_mlstm_vo]

    def tail_steps(t):
        return [lambda: t.branch(2), lambda: t.branch(0), lambda: t.branch(1), t.finish]

    for step in head_steps(tiles[0]):
        step()
    kv = (kprev_ref[...], vprev_ref[...])
    carry_tail = []
    for i, t in enumerate(tiles):
        own = [t.conv] + [functools.partial(t.gate, b, p) for b in range(N_BRANCH)
                          for p in range(D_MODEL // GATE_PART)]
        nxt = head_steps(tiles[i + 1]) if i + 1 < nsub else []
        fillers = carry_tail + own + nxt
        slots = 2 * t.nblk
        for slot in range(slots):
            j = slot // 2
            if slot % 2 == 0:
                t.stage1(j, kv)
            else:
                kv = t.stage2(j, kv)
            take = -(-len(fillers) // (slots - slot))
            for step in fillers[:take]:
                step()
            fillers = fillers[take:]
        carry_tail = tail_steps(t)
    for step in carry_tail:
        step()
    kprev_ref[...], vprev_ref[...] = kv
    conv_ref[0:SUBLANES, :] = conv_ref[tm:tm + SUBLANES, :]


ROPE_K = 8 * SUBLANES


def _rope_constants():
    half = ROT_DIM // 2
    inv_freq = ROPE_THETA ** (-np.arange(0, ROT_DIM, 2, dtype=np.float64) / ROT_DIM)
    freq = np.repeat(inv_freq[:, None], LANES, axis=1).astype(np.float32)
    l64 = np.arange(LANES) % HEAD_DIM
    expand = np.zeros((ROPE_K, 2 * LANES), np.float32)
    for f in range(half):
        cos_lanes = np.where((l64 < ROT_DIM) & (l64 % half == f), 1.0, 0.0)
        sin_lanes = np.where(l64 == f, -1.0, 0.0) + np.where(l64 == f + half, 1.0, 0.0)
        for piece in range(3):
            expand[piece * SUBLANES + f, :LANES] = cos_lanes
            expand[(3 + piece) * SUBLANES + f, LANES:] = sin_lanes
    expand[6 * SUBLANES, :LANES] = np.where(l64 >= ROT_DIM, 1.0, 0.0)
    return jnp.asarray(freq), jnp.asarray(expand, dtype=BF16)


def _pack_in_proj(w):
    offs = np.concatenate([[0], np.cumsum(SIZES)]).tolist()
    seg = [w[..., offs[i]:offs[i + 1]] for i in range(len(SIZES))]
    (aq, ak, av, mq, mk, mv, mo, mi, mf, cx, cb, cc, gates) = seg
    aq_heads = [aq[..., h * HEAD_DIM:(h + 1) * HEAD_DIM] for h in Q_HEAD_ORDER]
    zi = jnp.zeros(w.shape[:-1] + (LANES - MLSTM_HEADS,), w.dtype)
    zf = jnp.zeros(w.shape[:-1] + (LANES - 2 * MLSTM_HEADS,), w.dtype)
    return jnp.concatenate(aq_heads + [ak, av, mq, mk, mv, mo, cx, cb, cc, gates,
                                       mi, zi, mf, mf, zf], axis=-1)


def _pack_kernel(wt_ref, o_ref):
    offs = np.concatenate([[0], np.cumsum(SIZES)]).tolist()
    lane = lax.broadcasted_iota(jnp.int32, (1, LANES), 1)

    def put(lo, src):
        o_ref[:, lo:lo + src.shape[0]] = src.T.astype(BF16)

    for c in range(N_Q_HEADS // 2):
        pair = [wt_ref[h * HEAD_DIM:(h + 1) * HEAD_DIM, :] for h in Q_HEAD_ORDER[2 * c:2 * c + 2]]
        put(c * LANES, jnp.concatenate(pair, axis=0))
    for lo in range(OFF_AK, OFF_C, 4 * LANES):
        hi = min(lo + 4 * LANES, OFF_C)
        put(lo, wt_ref[lo:hi, :])
    shift = offs[9] - OFF_C
    for lo in range(OFF_C, OFF_IF, 4 * LANES):
        hi = min(lo + 4 * LANES, OFF_IF)
        put(lo, wt_ref[lo + shift:hi + shift, :])
    t = wt_ref[offs[7]:offs[7] + LANES, :].T
    o_ref[:, OFF_IF:OFF_IF + LANES] = jnp.where(lane < MLSTM_HEADS, t, 0.0).astype(BF16)
    f_lo = pltpu.roll(t, LANES - MLSTM_HEADS, axis=1)
    f_tile = jnp.where(lane < MLSTM_HEADS, f_lo, jnp.where(lane < 2 * MLSTM_HEADS, t, 0.0))
    o_ref[:, OFF_IF + LANES:P_PACKED] = f_tile.astype(BF16)


def _pack_in_proj_weights(w_in):
    depth, d, p_in = w_in.shape
    rows = 256
    return pl.pallas_call(
        _pack_kernel,
        out_shape=jax.ShapeDtypeStruct((depth, d, P_PACKED), BF16),
        grid=(depth, d // rows),
        in_specs=[pl.BlockSpec((None, p_in, rows), lambda l, i: (l, 0, i))],
        out_specs=pl.BlockSpec((None, rows, P_PACKED), lambda l, i: (l, i, 0)),
        compiler_params=pltpu.CompilerParams(
            dimension_semantics=("arbitrary", "arbitrary"),
            vmem_limit_bytes=VMEM_LIMIT_BYTES),
        name="pack_in_proj",
    )(jnp.swapaxes(w_in, 1, 2))


def _pack_branch(w_branch):
    wb0 = jnp.concatenate([w_branch[:, 0:1, h * HEAD_DIM:(h + 1) * HEAD_DIM, :] for h in Q_HEAD_ORDER],
                          axis=2)
    return jnp.concatenate([wb0, w_branch[:, 1:]], axis=1)


def _mixer_tile(seq):
    if seq % 512 == 0:
        return 512, 256
    if seq % 256 == 0:
        return 256, 256
    return ATTN_BLOCK, ATTN_BLOCK


def _mixer_layer(x, layer, positions, freq, expand, win, bin_, sinks, norm_w, conv_w, wb, wo, ln_w, ln_b):
    bsz, seq, d = x.shape
    tm, tsub = _mixer_tile(seq)
    ns = seq // tm
    nblk = tm // ATTN_BLOCK
    pos = positions.reshape(bsz * ns, nblk, ATTN_BLOCK)

    def const(shape, idx):
        return pl.BlockSpec(shape, lambda b, s, *_: idx, pipeline_mode=pl.Buffered(1))

    grid_spec = pltpu.PrefetchScalarGridSpec(
        num_scalar_prefetch=1,
        grid=(bsz, ns),
        in_specs=[
            pl.BlockSpec((1, tm, d), lambda b, s, *_: (b, s, 0)),
            pl.BlockSpec((1, nblk, ATTN_BLOCK), lambda b, s, *_: (b * ns + s, 0, 0)),
            const((SUBLANES, LANES), (0, 0)),
            const((ROPE_K, 2 * LANES), (0, 0)),
            const((None, d, P_PACKED), (layer, 0, 0)),
            const((None, 1, P_PACKED), (layer, 0, 0)),
            const((None, 1, MLSTM_WIDTH), (layer, 0, 0)),
            const((None, CONV_K, CONV_WIDTH), (layer, 0, 0)),
            const((None, N_BRANCH, BRANCH_WIDTH, d), (layer, 0, 0, 0)),
            const((None, d, d), (layer, 0, 0)),
            const((None, None, 1, d), (layer, 1, 0, 0)),
            const((None, None, 1, d), (layer, 1, 0, 0)),
        ],
        out_specs=pl.BlockSpec((1, tm, d), lambda b, s, *_: (b, s, 0)),
        scratch_shapes=[
            pltpu.VMEM((ATTN_BLOCK, LANES), BF16),
            pltpu.VMEM((ATTN_BLOCK, LANES), BF16),
            pltpu.VMEM((MLSTM_HEADS, MLSTM_HEAD_DIM, 2 * MLSTM_HEAD_DIM), F32),
            pltpu.VMEM((MLSTM_HEADS, SUBLANES, LANES), F32),
            pltpu.VMEM((tm + 2 * SUBLANES, CONV_WIDTH), F32),
            pltpu.VMEM((tm, N_BRANCH * BRANCH_WIDTH), BF16),
        ],
    )
    return pl.pallas_call(
        functools.partial(_mixer_kernel, tm=tm, tsub=tsub),
        out_shape=jax.ShapeDtypeStruct((bsz, seq, d), F32),
        grid_spec=grid_spec,
        compiler_params=pltpu.CompilerParams(
            dimension_semantics=("arbitrary", "arbitrary"),
            vmem_limit_bytes=VMEM_LIMIT_BYTES),
        name="token_mixer",
    )(sinks, x, pos, freq, expand, win, bin_, norm_w, conv_w, wb, wo, ln_w, ln_b)


def kernel(x, positions, w_in, b_in, attn_sinks, mlstm_norm_w, conv_w, w_branch, w_out,
           ffn1_w_gu, ffn1_w_down, ffn2_w_gu, ffn2_w_down, ln_w, ln_b):
    bsz, seq, d = x.shape
    depth = w_in.shape[0]
    ffn_w = ((ffn1_w_gu.astype(BF16), ffn1_w_down.astype(BF16)),
             (ffn2_w_gu.astype(BF16), ffn2_w_down.astype(BF16)))
    win = _pack_in_proj_weights(w_in)
    bin_ = _pack_in_proj(b_in)[:, None, :]
    wb = _pack_branch(w_branch.astype(BF16))
    wo = w_out.astype(BF16)
    norm_w = mlstm_norm_w[:, None, :]
    ln_w4 = ln_w[:, :, None, :]
    ln_b4 = ln_b[:, :, None, :]
    freq, expand = _rope_constants()
    for l in range(depth):
        x = _ffn_layer(x.reshape(bsz * seq, d), l, ffn_w[0][0], ffn_w[0][1], ln_w4, ln_b4, 0
                       ).reshape(bsz, seq, d)
        x = _mixer_layer(x, l, positions, freq, expand, win, bin_, attn_sinks[l], norm_w, conv_w, wb, wo,
                         ln_w4, ln_b4)
        x = _ffn_layer(x.reshape(bsz * seq, d), l, ffn_w[1][0], ffn_w[1][1], ln_w4, ln_b4, 2
                       ).reshape(bsz, seq, d)
    return x
```

```python
import functools
import math

import numpy as np
import jax
import jax.numpy as jnp
from jax import lax
from jax.experimental import pallas as pl
from jax.experimental.pallas import tpu as pltpu

D_MODEL = 1024
DEPTH = 2

HEAD_DIM = 64
N_Q_HEADS = 8
N_KV_HEADS = 2
GROUP = N_Q_HEADS // N_KV_HEADS
WINDOW = 128
ATTN_BLOCK = 128
ROPE_THETA = 500000.0
ROT_DIM = HEAD_DIM // 4
ATTN_WIDTH = N_Q_HEADS * HEAD_DIM
KV_WIDTH = N_KV_HEADS * HEAD_DIM

MLSTM_HEADS = 4
MLSTM_HEAD_DIM = D_MODEL // 8
MLSTM_WIDTH = MLSTM_HEADS * MLSTM_HEAD_DIM
MLSTM_CHUNK = 128

CONV_WIDTH = D_MODEL // 2
CONV_K = 3

N_BRANCH = 3
BRANCH_WIDTH = 512
D_FF = 2816
LN_EPS = 1e-5
ALPHA = (2.0 * DEPTH) ** 0.25
NEG_INF = -1e30

SIZES = (ATTN_WIDTH, KV_WIDTH, KV_WIDTH,
         MLSTM_WIDTH, MLSTM_WIDTH, MLSTM_WIDTH, MLSTM_WIDTH, MLSTM_HEADS, MLSTM_HEADS,
         CONV_WIDTH, CONV_WIDTH, CONV_WIDTH,
         N_BRANCH * D_MODEL)

LANES = 128
SUBLANES = 8
VMEM_LIMIT_BYTES = 56 * 1024 * 1024

OFF_AQ = 0
OFF_AK = OFF_AQ + ATTN_WIDTH
OFF_AV = OFF_AK + KV_WIDTH
OFF_M = OFF_AV + KV_WIDTH
OFF_C = OFF_M + 4 * MLSTM_WIDTH
OFF_G = OFF_C + 3 * CONV_WIDTH
OFF_IF = OFF_G + N_BRANCH * D_MODEL
P_PACKED = OFF_IF + 2 * LANES
GATE_PART = D_MODEL

Q_HEAD_ORDER = (0, 4, 1, 5, 2, 6, 3, 7)

BF16 = jnp.bfloat16
F32 = jnp.float32


def _dot(a, b):
    return jnp.dot(a, b, preferred_element_type=F32)


def _dot_nt(a, b):
    return lax.dot_general(a, b, (((1,), (1,)), ((), ())), preferred_element_type=F32)


def _dot_tn(a, b):
    return lax.dot_general(a, b, (((0,), (0,)), ((), ())), preferred_element_type=F32)


def _layer_norm_rows(y, w, b):
    mu = jnp.mean(y, axis=-1, keepdims=True)
    yc = y - mu
    var = jnp.mean(yc * yc, axis=-1, keepdims=True)
    return yc * lax.rsqrt(var + LN_EPS) * w + b


def _sigmoid(x):
    return 0.5 * jnp.tanh(0.5 * x) + 0.5


def _log_sigmoid(x):
    return jnp.minimum(x, 0.0) - jnp.log1p(jnp.exp(-jnp.abs(x)))


def _ffn_kernel(x_ref, wgu_hbm, wd_hbm, lnw_ref, lnb_ref, o_ref,
                wgu_ref, wd_ref, stage_gu_ref, stage_d_ref, sem, *, chunks, layer):
    i = pl.program_id(0)
    d_ff = wd_ref.shape[0]

    def chunk_copies(k, slot):
        c0, c1 = chunks[k]
        return (
            pltpu.make_async_copy(wgu_hbm.at[layer, :, c0:c1], stage_gu_ref.at[slot, 0], sem.at[slot, 0]),
            pltpu.make_async_copy(wgu_hbm.at[layer, :, d_ff + c0:d_ff + c1], stage_gu_ref.at[slot, 1],
                                  sem.at[slot, 1]),
            pltpu.make_async_copy(wd_hbm.at[layer, c0:c1, :], stage_d_ref.at[slot], sem.at[slot, 2]),
        )

    def run(load_weights):
        x = x_ref[...]
        xb = x.astype(BF16)
        acc = None
        if load_weights:
            for cp in chunk_copies(0, 0):
                cp.start()
        for k, (c0, c1) in enumerate(chunks):
            if load_weights:
                slot = k % 2
                if k + 1 < len(chunks):
                    for cp in chunk_copies(k + 1, 1 - slot):
                        cp.start()
                for cp in chunk_copies(k, slot):
                    cp.wait()
                wgu_ref[:, c0:c1] = stage_gu_ref[slot, 0].astype(BF16)
                wgu_ref[:, d_ff + c0:d_ff + c1] = stage_gu_ref[slot, 1].astype(BF16)
                wd_ref[c0:c1, :] = stage_d_ref[slot].astype(BF16)
            g = _dot(xb, wgu_ref[:, c0:c1])
            u = _dot(xb, wgu_ref[:, d_ff + c0:d_ff + c1])
            a = (g * _sigmoid(g) * u).astype(BF16)
            part = _dot(a, wd_ref[c0:c1, :])
            acc = part if acc is None else acc + part
        y = ALPHA * x + 0.5 * acc
        o_ref[...] = _layer_norm_rows(y, lnw_ref[...], lnb_ref[...])

    @pl.when(i == 0)
    def _first():
        run(True)

    @pl.when(i > 0)
    def _rest():
        run(False)


def _ffn_tiles(n_tokens, d_ff):
    tm = 512 if n_tokens % 512 == 0 else 128
    step = 2 * LANES
    assert d_ff % step == 0
    chunks = tuple((c, c + step) for c in range(0, d_ff, step))
    return tm, chunks


def _ffn_layer(x2d, layer, w_gu, w_down, ln_w, ln_b, ln_idx):
    n, d = x2d.shape
    d_ff = w_down.shape[1]
    tm, chunks = _ffn_tiles(n, d_ff)
    step = chunks[0][1] - chunks[0][0]

    def const(shape, idx):
        return pl.BlockSpec(shape, lambda i: idx, pipeline_mode=pl.Buffered(1))

    return pl.pallas_call(
        functools.partial(_ffn_kernel, chunks=chunks, layer=layer),
        out_shape=jax.ShapeDtypeStruct((n, d), F32),
        grid=(n // tm,),
        in_specs=[
            pl.BlockSpec((tm, d), lambda i: (i, 0)),
            pl.BlockSpec(memory_space=pl.ANY),
            pl.BlockSpec(memory_space=pl.ANY),
            const((None, None, 1, d), (layer, ln_idx, 0, 0)),
            const((None, None, 1, d), (layer, ln_idx, 0, 0)),
        ],
        out_specs=pl.BlockSpec((tm, d), lambda i: (i, 0)),
        scratch_shapes=[
            pltpu.VMEM((d, 2 * d_ff), BF16),
            pltpu.VMEM((d_ff, d), BF16),
            pltpu.VMEM((2, 2, d, step), F32),
            pltpu.VMEM((2, step, d), F32),
            pltpu.SemaphoreType.DMA((2, 3)),
        ],
        compiler_params=pltpu.CompilerParams(
            dimension_semantics=("arbitrary",),
            vmem_limit_bytes=VMEM_LIMIT_BYTES),
        name="swiglu_ln",
    )(x2d, w_gu, w_down, ln_w, ln_b)


def _rope_tables(pos_row, freq, expand):
    ang = freq * pos_row.astype(F32)
    pieces = []
    for t in (jnp.cos(ang), jnp.sin(ang)):
        hi = t.astype(BF16)
        r1 = t - hi.astype(F32)
        mid = r1.astype(BF16)
        lo = (r1 - mid.astype(F32)).astype(BF16)
        pieces += [hi, mid, lo]
    pieces += [jnp.ones((SUBLANES, LANES), BF16), jnp.zeros((SUBLANES, LANES), BF16)]
    cs = _dot_tn(jnp.concatenate(pieces, axis=0), expand)
    return cs[:, 0:LANES], cs[:, LANES:]


def _rope(t, cos_f, sin_f):
    half = ROT_DIM // 2
    lane = lax.broadcasted_iota(jnp.int32, (1, LANES), 1)
    up = pltpu.roll(t, LANES - half, axis=1)
    down = pltpu.roll(t, half, axis=1)
    return t * cos_f + jnp.where((lane & (HEAD_DIM - 1)) < half, up, down) * sin_f


def _split_kv(prev, cur):
    lane = lax.broadcasted_iota(jnp.int32, (1, LANES), 1)
    half0 = lane < HEAD_DIM
    zero = jnp.zeros((), BF16)
    both = jnp.concatenate([prev, cur], axis=0)
    return jnp.concatenate([jnp.where(half0, both, zero),
                            jnp.where(half0, zero, both)], axis=0)


def _attention_scores(q_all, k_cur, k_prev):
    return _dot_nt(q_all, _split_kv(k_prev, k_cur))


def _attention_softmax(s_all, thr, sinks_ref):
    L = ATTN_BLOCK
    lane = lax.broadcasted_iota(jnp.int32, (1, LANES), 1)
    half0 = lane < HEAD_DIM
    qi = lax.broadcasted_iota(jnp.int32, (L, 2 * L), 0)
    kj = lax.broadcasted_iota(jnp.int32, (L, 2 * L), 1)
    rel = qi + L - kj
    mask = (rel >= 0) & (rel < WINDOW) & (kj >= thr)
    ps, scales = [], []
    for c in range(N_Q_HEADS // 2):
        pg, inv = [], []
        for g in range(N_KV_HEADS):
            sink = sinks_ref[Q_HEAD_ORDER[2 * c + g]]
            sg = jnp.where(mask, s_all[c * L:(c + 1) * L, g * 2 * L:(g + 1) * 2 * L], NEG_INF)
            m = jnp.maximum(jnp.max(sg, axis=-1, keepdims=True), sink)
            p = jnp.exp(sg - m)
            den = jnp.sum(p, axis=-1, keepdims=True) + jnp.exp(sink - m)
            pg.append(p.astype(BF16))
            inv.append(1.0 / den)
        ps.append(jnp.concatenate(pg, axis=1))
        scales.append(jnp.where(half0, inv[0], inv[1]))
    return jnp.concatenate(ps, axis=0), jnp.concatenate(scales, axis=0)


def _mlstm_gates(gif, row0):
    L = MLSTM_CHUNK
    rows = slice(row0, row0 + L)
    lane = lax.broadcasted_iota(jnp.int32, (1, LANES), 1)
    gi = gif[rows, 0:LANES]
    gf = gif[rows, LANES:2 * LANES]
    lf = jnp.where(lane < 2 * MLSTM_HEADS, _log_sigmoid(gf), 0.0)
    r_i = lax.broadcasted_iota(jnp.int32, (L, L), 0)
    c_i = lax.broadcasted_iota(jnp.int32, (L, L), 1)
    tril = jnp.where(r_i >= c_i, 1.0, 0.0).astype(BF16)
    hi = lf.astype(BF16)
    r1 = lf - hi.astype(F32)
    mid = r1.astype(BF16)
    lo = (r1 - mid.astype(F32)).astype(BF16)
    bsum = _dot(tril, jnp.concatenate([hi, mid, lo], axis=1))
    bsum = bsum[:, 0:LANES] + bsum[:, LANES:2 * LANES] + bsum[:, 2 * LANES:]
    t_col = jnp.where(lane < MLSTM_HEADS, gi - bsum, bsum)
    return t_col, t_col.T


def _mlstm_stage1(mqk, t_col, t_row, s_ref, m_ref, row0):
    L = MLSTM_CHUNK
    dh = MLSTM_HEAD_DIM
    rows = slice(row0, row0 + L)
    r_i = lax.broadcasted_iota(jnp.int32, (L, L), 0)
    c_i = lax.broadcasted_iota(jnp.int32, (L, L), 1)
    causal = r_i >= c_i
    scale = dh ** -0.5
    heads = []
    for h in range(MLSTM_HEADS):
        q = mqk[rows, h * dh:(h + 1) * dh].astype(BF16)
        k = mqk[rows, MLSTM_WIDTH + h * dh:MLSTM_WIDTH + (h + 1) * dh] * scale
        state = s_ref[h]
        qk = _dot_nt(q, k.astype(BF16))
        from_state = _dot(q, state.astype(BF16))
        heads.append(dict(k=k, state=state, qk=qk, from_state=from_state))
    for h, hd in enumerate(heads):
        b_col = t_col[:, MLSTM_HEADS + h:MLSTM_HEADS + h + 1]
        d_col = t_col[:, h:h + 1]
        d_row = t_row[h:h + 1, :]
        m_prev = m_ref[h][0:1, 0:1]
        logw = jnp.where(causal, b_col + d_row, NEG_INF)
        inter = b_col + m_prev
        m_t = jnp.maximum(inter, jnp.max(logw, axis=-1, keepdims=True))
        scores = hd["qk"] * jnp.exp(logw - m_t)
        m_new = m_t[L - 1:L, :]
        b_last = b_col[L - 1:L, :]
        ws = jnp.exp(b_last + d_col - m_new)
        hd.update(m_t=m_t, w_inter=jnp.exp(inter - m_t), scores=scores,
                  row_sum=jnp.sum(scores, axis=-1, keepdims=True),
                  decay=jnp.exp(b_last + m_prev - m_new), m_new=m_new,
                  kw=(hd["k"] * ws).astype(BF16))
    return heads


def _mlstm_stage2(mvo, heads, s_ref, m_ref, nw_ref, row0):
    L = MLSTM_CHUNK
    dh = MLSTM_HEAD_DIM
    rows = slice(row0, row0 + L)
    lane = lax.broadcasted_iota(jnp.int32, (1, LANES), 1)
    ones_tile = jnp.broadcast_to(jnp.where(lane == 0, 1.0, 0.0).astype(BF16), (L, LANES))
    outs = []
    for h, hd in enumerate(heads):
        v = mvo[rows, h * dh:(h + 1) * dh].astype(BF16)
        hd["intra"] = _dot(hd["scores"].astype(BF16), v)
        v_ext = jnp.concatenate([v, ones_tile], axis=1)
        s_ref[h] = hd["decay"] * hd["state"] + _dot_tn(hd["kw"], v_ext)
        m_ref[h] = jnp.broadcast_to(hd["m_new"], (SUBLANES, LANES))
    for h, hd in enumerate(heads):
        og = mvo[rows, MLSTM_WIDTH + h * dh:MLSTM_WIDTH + (h + 1) * dh]
        num = hd["w_inter"] * hd["from_state"][:, 0:dh] + hd["intra"]
        den = hd["w_inter"] * hd["from_state"][:, dh:dh + 1] + hd["row_sum"]
        hh = num / jnp.maximum(jnp.abs(den), jnp.exp(-hd["m_t"]))
        mu = jnp.mean(hh, axis=-1, keepdims=True)
        hc = hh - mu
        hn = hc * lax.rsqrt(jnp.mean(hc * hc, axis=-1, keepdims=True) + LN_EPS)
        outs.append(hn * nw_ref[:, h * dh:(h + 1) * dh] * _sigmoid(og))
    return outs


class _SubTile:
    def __init__(self, sub, tsub, s_idx, refs):
        self.sub, self.tsub, self.s_idx, self.r = sub, tsub, s_idx, refs
        self.r0 = sub * tsub
        self.nblk = tsub // ATTN_BLOCK
        self.x = refs["x"][0, self.r0:self.r0 + tsub, :]
        self.xb = self.x.astype(BF16)
        self.gates = [[None] * (D_MODEL // GATE_PART) for _ in range(N_BRANCH)]
        self.merged = None

    def proj(self, lo, hi):
        return _dot(self.xb, self.r["win"][:, lo:hi]) + self.r["bin"][:, lo:hi]

    def ropes(self):
        r, n = self.r, self.nblk
        self.rope = [_rope_tables(r["pos"][0, self.sub * n + j:self.sub * n + j + 1, :],
                                  r["freq"][...], r["expand"][...]) for j in range(n)]

    def proj_attn(self):
        self.qkv = self.proj(OFF_AQ, OFF_M)

    def proj_gates(self):
        gif = self.proj(OFF_IF, P_PACKED)
        self.mgates = [_mlstm_gates(gif, j * MLSTM_CHUNK) for j in range(self.nblk)]

    def proj_mlstm_qk(self):
        self.mqk = self.proj(OFF_M, OFF_M + 2 * MLSTM_WIDTH)

    def proj_mlstm_vo(self):
        self.mvo = self.proj(OFF_M + 2 * MLSTM_WIDTH, OFF_C)

    def conv(self):
        r, r0, tsub = self.r, self.r0, self.tsub
        cproj = self.proj(OFF_C, OFF_G)
        u = cproj[:, 2 * CONV_WIDTH:] * cproj[:, 0:CONV_WIDTH]
        r["conv"][SUBLANES + r0:SUBLANES + r0 + tsub, :] = u
        u1 = r["conv"][pl.ds(SUBLANES + r0 - 1, tsub), :]
        u2 = r["conv"][pl.ds(SUBLANES + r0 - 2, tsub), :]
        yc = r["cw"][0:1, :] * u2 + r["cw"][1:2, :] * u1 + r["cw"][2:3, :] * u
        r["y"][r0:r0 + tsub, 2 * BRANCH_WIDTH:] = (cproj[:, CONV_WIDTH:2 * CONV_WIDTH] * yc).astype(BF16)

    def gate(self, b, part):
        lo = OFF_G + b * D_MODEL + part * GATE_PART
        self.gates[b][part] = _sigmoid(self.proj(lo, lo + GATE_PART))

    def stage1(self, j, kv):
        L = ATTN_BLOCK
        rows = slice(j * L, (j + 1) * L)
        cos_f, sin_f = self.rope[j]
        q_all = jnp.concatenate(
            [(_rope(self.qkv[rows, c * LANES:(c + 1) * LANES], cos_f, sin_f)
              * (HEAD_DIM ** -0.5)).astype(BF16) for c in range(N_Q_HEADS // 2)], axis=0)
        self.k_cur = _rope(self.qkv[rows, OFF_AK:OFF_AK + LANES], cos_f, sin_f).astype(BF16)
        self.v_cur = self.qkv[rows, OFF_AV:OFF_AV + LANES].astype(BF16)
        thr = jnp.where(self.s_idx == 0, L, 0) if (self.sub == 0 and j == 0) else 0
        s_all = _attention_scores(q_all, self.k_cur, kv[0])
        self.heads = _mlstm_stage1(self.mqk, self.mgates[j][0], self.mgates[j][1],
                                   self.r["s"], self.r["m"], j * L)
        self.p_all, self.o_scale = _attention_softmax(s_all, thr, self.r["sinks"])

    def stage2(self, j, kv):
        L = ATTN_BLOCK
        r = self.r
        yrows = slice(self.r0 + j * L, self.r0 + (j + 1) * L)
        o_all = _dot(self.p_all, _split_kv(kv[1], self.v_cur)) * self.o_scale
        for c in range(N_Q_HEADS // 2):
            r["y"][yrows, c * LANES:(c + 1) * LANES] = o_all[c * L:(c + 1) * L, :].astype(BF16)
        outs = _mlstm_stage2(self.mvo, self.heads, r["s"], r["m"], r["nw"], j * L)
        for h, o in enumerate(outs):
            r["y"][yrows, BRANCH_WIDTH + h * MLSTM_HEAD_DIM:
                   BRANCH_WIDTH + (h + 1) * MLSTM_HEAD_DIM] = o.astype(BF16)
        return self.k_cur, self.v_cur

    def branch(self, b):
        r, r0, tsub = self.r, self.r0, self.tsub
        term = jnp.concatenate(self.gates[b], axis=1) * _dot(
            r["y"][r0:r0 + tsub, b * BRANCH_WIDTH:(b + 1) * BRANCH_WIDTH], r["wb"][b])
        self.merged = term if self.merged is None else self.merged + term

    def finish(self):
        r = self.r
        mix = _dot(self.merged.astype(BF16), r["wo"][...])
        y = ALPHA * self.x + mix
        r["o"][0, self.r0:self.r0 + self.tsub, :] = _layer_norm_rows(y, r["lnw"][...], r["lnb"][...])


def _mixer_kernel(sinks_ref, x_ref, pos_ref, freq_ref, expand_ref, win_ref, bin_ref, nw_ref, cw_ref,
                  wb_ref, wo_ref, lnw_ref, lnb_ref, o_ref,
                  kprev_ref, vprev_ref, s_ref, m_ref, conv_ref, y_ref, *, tm, tsub):
    s_idx = pl.program_id(1)

    @pl.when(s_idx == 0)
    def _reset():
        kprev_ref[...] = jnp.zeros_like(kprev_ref)
        vprev_ref[...] = jnp.zeros_like(vprev_ref)
        s_ref[...] = jnp.zeros_like(s_ref)
        m_ref[...] = jnp.zeros_like(m_ref)
        conv_ref[0:SUBLANES, :] = jnp.zeros((SUBLANES, CONV_WIDTH), F32)

    refs = dict(sinks=sinks_ref, x=x_ref, pos=pos_ref, freq=freq_ref, expand=expand_ref, win=win_ref,
                bin=bin_ref, nw=nw_ref, cw=cw_ref, wb=wb_ref, wo=wo_ref, lnw=lnw_ref, lnb=lnb_ref,
                o=o_ref, s=s_ref, m=m_ref, conv=conv_ref, y=y_ref)
    nsub = tm // tsub
    tiles = [_SubTile(t, tsub, s_idx, refs) for t in range(nsub)]

    def head_steps(t):
        return [t.ropes, t.proj_attn, t.proj_gates, t.proj_mlstm_qk, t.proj_mlstm_vo]

    def tail_steps(t):
        return [lambda: t.branch(2), lambda: t.branch(0), lambda: t.branch(1), t.finish]

    for step in head_steps(tiles[0]):
        step()
    kv = (kprev_ref[...], vprev_ref[...])
    carry_tail = []
    for i, t in enumerate(tiles):
        own = [t.conv] + [functools.partial(t.gate, b, p) for b in range(N_BRANCH)
                          for p in range(D_MODEL // GATE_PART)]
        nxt = head_steps(tiles[i + 1]) if i + 1 < nsub else []
        fillers = carry_tail + own + nxt
        slots = 2 * t.nblk
        for slot in range(slots):
            j = slot // 2
            if slot % 2 == 0:
                t.stage1(j, kv)
            else:
                kv = t.stage2(j, kv)
            take = -(-len(fillers) // (slots - slot))
            for step in fillers[:take]:
                step()
            fillers = fillers[take:]
        carry_tail = tail_steps(t)
    for step in carry_tail:
        step()
    kprev_ref[...], vprev_ref[...] = kv
    conv_ref[0:SUBLANES, :] = conv_ref[tm:tm + SUBLANES, :]


ROPE_K = 8 * SUBLANES


def _rope_constants():
    half = ROT_DIM // 2
    inv_freq = ROPE_THETA ** (-np.arange(0, ROT_DIM, 2, dtype=np.float64) / ROT_DIM)
    freq = np.repeat(inv_freq[:, None], LANES, axis=1).astype(np.float32)
    l64 = np.arange(LANES) % HEAD_DIM
    expand = np.zeros((ROPE_K, 2 * LANES), np.float32)
    for f in range(half):
        cos_lanes = np.where((l64 < ROT_DIM) & (l64 % half == f), 1.0, 0.0)
        sin_lanes = np.where(l64 == f, -1.0, 0.0) + np.where(l64 == f + half, 1.0, 0.0)
        for piece in range(3):
            expand[piece * SUBLANES + f, :LANES] = cos_lanes
            expand[(3 + piece) * SUBLANES + f, LANES:] = sin_lanes
    expand[6 * SUBLANES, :LANES] = np.where(l64 >= ROT_DIM, 1.0, 0.0)
    return jnp.asarray(freq), jnp.asarray(expand, dtype=BF16)


def _pack_in_proj(w):
    offs = np.concatenate([[0], np.cumsum(SIZES)]).tolist()
    seg = [w[..., offs[i]:offs[i + 1]] for i in range(len(SIZES))]
    (aq, ak, av, mq, mk, mv, mo, mi, mf, cx, cb, cc, gates) = seg
    aq_heads = [aq[..., h * HEAD_DIM:(h + 1) * HEAD_DIM] for h in Q_HEAD_ORDER]
    zi = jnp.zeros(w.shape[:-1] + (LANES - MLSTM_HEADS,), w.dtype)
    zf = jnp.zeros(w.shape[:-1] + (LANES - 2 * MLSTM_HEADS,), w.dtype)
    return jnp.concatenate(aq_heads + [ak, av, mq, mk, mv, mo, cx, cb, cc, gates,
                                       mi, zi, mf, mf, zf], axis=-1)


def _pack_kernel(wt_ref, o_ref):
    offs = np.concatenate([[0], np.cumsum(SIZES)]).tolist()
    lane = lax.broadcasted_iota(jnp.int32, (1, LANES), 1)

    def put(lo, src):
        o_ref[:, lo:lo + src.shape[0]] = src.T.astype(BF16)

    for c in range(N_Q_HEADS // 2):
        pair = [wt_ref[h * HEAD_DIM:(h + 1) * HEAD_DIM, :] for h in Q_HEAD_ORDER[2 * c:2 * c + 2]]
        put(c * LANES, jnp.concatenate(pair, axis=0))
    for lo in range(OFF_AK, OFF_C, 4 * LANES):
        hi = min(lo + 4 * LANES, OFF_C)
        put(lo, wt_ref[lo:hi, :])
    shift = offs[9] - OFF_C
    for lo in range(OFF_C, OFF_IF, 4 * LANES):
        hi = min(lo + 4 * LANES, OFF_IF)
        put(lo, wt_ref[lo + shift:hi + shift, :])
    t = wt_ref[offs[7]:offs[7] + LANES, :].T
    o_ref[:, OFF_IF:OFF_IF + LANES] = jnp.where(lane < MLSTM_HEADS, t, 0.0).astype(BF16)
    f_lo = pltpu.roll(t, LANES - MLSTM_HEADS, axis=1)
    f_tile = jnp.where(lane < MLSTM_HEADS, f_lo, jnp.where(lane < 2 * MLSTM_HEADS, t, 0.0))
    o_ref[:, OFF_IF + LANES:P_PACKED] = f_tile.astype(BF16)


def _pack_in_proj_weights(w_in):
    depth, d, p_in = w_in.shape
    rows = 256
    return pl.pallas_call(
        _pack_kernel,
        out_shape=jax.ShapeDtypeStruct((depth, d, P_PACKED), BF16),
        grid=(depth, d // rows),
        in_specs=[pl.BlockSpec((None, p_in, rows), lambda l, i: (l, 0, i))],
        out_specs=pl.BlockSpec((None, rows, P_PACKED), lambda l, i: (l, i, 0)),
        compiler_params=pltpu.CompilerParams(
            dimension_semantics=("arbitrary", "arbitrary"),
            vmem_limit_bytes=VMEM_LIMIT_BYTES),
        name="pack_in_proj",
    )(jnp.swapaxes(w_in, 1, 2))


def _pack_branch(w_branch):
    wb0 = jnp.concatenate([w_branch[:, 0:1, h * HEAD_DIM:(h + 1) * HEAD_DIM, :] for h in Q_HEAD_ORDER],
                          axis=2)
    return jnp.concatenate([wb0, w_branch[:, 1:]], axis=1)


def _mixer_tile(seq):
    if seq % 512 == 0:
        return 512, 256
    if seq % 256 == 0:
        return 256, 256
    return ATTN_BLOCK, ATTN_BLOCK


def _mixer_layer(x, layer, positions, freq, expand, win, bin_, sinks, norm_w, conv_w, wb, wo, ln_w, ln_b):
    bsz, seq, d = x.shape
    tm, tsub = _mixer_tile(seq)
    ns = seq // tm
    nblk = tm // ATTN_BLOCK
    pos = positions.reshape(bsz * ns, nblk, ATTN_BLOCK)

    def const(shape, idx):
        return pl.BlockSpec(shape, lambda b, s, *_: idx, pipeline_mode=pl.Buffered(1))

    grid_spec = pltpu.PrefetchScalarGridSpec(
        num_scalar_prefetch=1,
        grid=(bsz, ns),
        in_specs=[
            pl.BlockSpec((1, tm, d), lambda b, s, *_: (b, s, 0)),
            pl.BlockSpec((1, nblk, ATTN_BLOCK), lambda b, s, *_: (b * ns + s, 0, 0)),
            const((SUBLANES, LANES), (0, 0)),
            const((ROPE_K, 2 * LANES), (0, 0)),
            const((None, d, P_PACKED), (layer, 0, 0)),
            const((None, 1, P_PACKED), (layer, 0, 0)),
            const((None, 1, MLSTM_WIDTH), (layer, 0, 0)),
            const((None, CONV_K, CONV_WIDTH), (layer, 0, 0)),
            const((None, N_BRANCH, BRANCH_WIDTH, d), (layer, 0, 0, 0)),
            const((None, d, d), (layer, 0, 0)),
            const((None, None, 1, d), (layer, 1, 0, 0)),
            const((None, None, 1, d), (layer, 1, 0, 0)),
        ],
        out_specs=pl.BlockSpec((1, tm, d), lambda b, s, *_: (b, s, 0)),
        scratch_shapes=[
            pltpu.VMEM((ATTN_BLOCK, LANES), BF16),
            pltpu.VMEM((ATTN_BLOCK, LANES), BF16),
            pltpu.VMEM((MLSTM_HEADS, MLSTM_HEAD_DIM, 2 * MLSTM_HEAD_DIM), F32),
            pltpu.VMEM((MLSTM_HEADS, SUBLANES, LANES), F32),
            pltpu.VMEM((tm + 2 * SUBLANES, CONV_WIDTH), F32),
            pltpu.VMEM((tm, N_BRANCH * BRANCH_WIDTH), BF16),
        ],
    )
    return pl.pallas_call(
        functools.partial(_mixer_kernel, tm=tm, tsub=tsub),
        out_shape=jax.ShapeDtypeStruct((bsz, seq, d), F32),
        grid_spec=grid_spec,
        compiler_params=pltpu.CompilerParams(
            dimension_semantics=("arbitrary", "arbitrary"),
            vmem_limit_bytes=VMEM_LIMIT_BYTES),
        name="token_mixer",
    )(sinks, x, pos, freq, expand, win, bin_, norm_w, conv_w, wb, wo, ln_w, ln_b)


def kernel(x, positions, w_in, b_in, attn_sinks, mlstm_norm_w, conv_w, w_branch, w_out,
           ffn1_w_gu, ffn1_w_down, ffn2_w_gu, ffn2_w_down, ln_w, ln_b):
    bsz, seq, d = x.shape
    depth = w_in.shape[0]
    ffn_w = ((ffn1_w_gu, ffn1_w_down), (ffn2_w_gu, ffn2_w_down))
    win = _pack_in_proj_weights(w_in)
    bin_ = _pack_in_proj(b_in)[:, None, :]
    wb = _pack_branch(w_branch.astype(BF16))
    wo = w_out.astype(BF16)
    norm_w = mlstm_norm_w[:, None, :]
    ln_w4 = ln_w[:, :, None, :]
    ln_b4 = ln_b[:, :, None, :]
    freq, expand = _rope_constants()
    for l in range(depth):
        x = _ffn_layer(x.reshape(bsz * seq, d), l, ffn_w[0][0], ffn_w[0][1], ln_w4, ln_b4, 0
                       ).reshape(bsz, seq, d)
        x = _mixer_layer(x, l, positions, freq, expand, win, bin_, attn_sinks[l], norm_w, conv_w, wb, wo,
                         ln_w4, ln_b4)
        x = _ffn_layer(x.reshape(bsz * seq, d), l, ffn_w[1][0], ffn_w[1][1], ln_w4, ln_b4, 2
                       ).reshape(bsz, seq, d)
    return x
```

```python
import functools
import math

import numpy as np
import jax
import jax.numpy as jnp
from jax import lax
from jax.experimental import pallas as pl
from jax.experimental.pallas import tpu as pltpu

D_MODEL = 1024
DEPTH = 2

HEAD_DIM = 64
N_Q_HEADS = 8
N_KV_HEADS = 2
GROUP = N_Q_HEADS // N_KV_HEADS
WINDOW = 128
ATTN_BLOCK = 128
ROPE_THETA = 500000.0
ROT_DIM = HEAD_DIM // 4
ATTN_WIDTH = N_Q_HEADS * HEAD_DIM
KV_WIDTH = N_KV_HEADS * HEAD_DIM

MLSTM_HEADS = 4
MLSTM_HEAD_DIM = D_MODEL // 8
MLSTM_WIDTH = MLSTM_HEADS * MLSTM_HEAD_DIM
MLSTM_CHUNK = 128

CONV_WIDTH = D_MODEL // 2
CONV_K = 3

N_BRANCH = 3
BRANCH_WIDTH = 512
D_FF = 2816
LN_EPS = 1e-5
ALPHA = (2.0 * DEPTH) ** 0.25
NEG_INF = -1e30

SIZES = (ATTN_WIDTH, KV_WIDTH, KV_WIDTH,
         MLSTM_WIDTH, MLSTM_WIDTH, MLSTM_WIDTH, MLSTM_WIDTH, MLSTM_HEADS, MLSTM_HEADS,
         CONV_WIDTH, CONV_WIDTH, CONV_WIDTH,
         N_BRANCH * D_MODEL)

LANES = 128
SUBLANES = 8
VMEM_LIMIT_BYTES = 56 * 1024 * 1024

OFF_AQ = 0
OFF_AK = OFF_AQ + ATTN_WIDTH
OFF_AV = OFF_AK + KV_WIDTH
OFF_M = OFF_AV + KV_WIDTH
OFF_C = OFF_M + 4 * MLSTM_WIDTH
OFF_G = OFF_C + 3 * CONV_WIDTH
OFF_IF = OFF_G + N_BRANCH * D_MODEL
P_PACKED = OFF_IF + 2 * LANES
GATE_PART = D_MODEL

Q_HEAD_ORDER = (0, 4, 1, 5, 2, 6, 3, 7)

BF16 = jnp.bfloat16
F32 = jnp.float32


def _dot(a, b):
    return jnp.dot(a, b, preferred_element_type=F32)


def _dot_nt(a, b):
    return lax.dot_general(a, b, (((1,), (1,)), ((), ())), preferred_element_type=F32)


def _dot_tn(a, b):
    return lax.dot_general(a, b, (((0,), (0,)), ((), ())), preferred_element_type=F32)


def _layer_norm_rows(y, w, b):
    mu = jnp.mean(y, axis=-1, keepdims=True)
    yc = y - mu
    var = jnp.mean(yc * yc, axis=-1, keepdims=True)
    return yc * lax.rsqrt(var + LN_EPS) * w + b


def _sigmoid(x):
    return 0.5 * jnp.tanh(0.5 * x) + 0.5


def _log_sigmoid(x):
    return jnp.minimum(x, 0.0) - jnp.log1p(jnp.exp(-jnp.abs(x)))


def _ffn_kernel(x_ref, wgu_hbm, wd_hbm, lnw_ref, lnb_ref, o_ref,
                wgu_ref, wd_ref, stage_gu_ref, stage_d_ref, sem, *, chunks, layer, tsub):
    i = pl.program_id(0)
    d_ff = wd_ref.shape[0]

    def chunk_copies(k, slot):
        c0, c1 = chunks[k]
        return (
            pltpu.make_async_copy(wgu_hbm.at[layer, :, c0:c1], stage_gu_ref.at[slot, 0], sem.at[slot, 0]),
            pltpu.make_async_copy(wgu_hbm.at[layer, :, d_ff + c0:d_ff + c1], stage_gu_ref.at[slot, 1],
                                  sem.at[slot, 1]),
            pltpu.make_async_copy(wd_hbm.at[layer, c0:c1, :], stage_d_ref.at[slot], sem.at[slot, 2]),
        )

    def run(load_weights):
        nsub = x_ref.shape[0] // tsub
        if load_weights:
            for cp in chunk_copies(0, 0):
                cp.start()
        pending = None
        for t in range(nsub):
            rows = slice(t * tsub, (t + 1) * tsub)
            x = x_ref[rows, :]
            xb = x.astype(BF16)
            acc = None
            for k, (c0, c1) in enumerate(chunks):
                if load_weights and t == 0:
                    slot = k % 2
                    if k + 1 < len(chunks):
                        for cp in chunk_copies(k + 1, 1 - slot):
                            cp.start()
                    for cp in chunk_copies(k, slot):
                        cp.wait()
                    wgu_ref[:, c0:c1] = stage_gu_ref[slot, 0].astype(BF16)
                    wgu_ref[:, d_ff + c0:d_ff + c1] = stage_gu_ref[slot, 1].astype(BF16)
                    wd_ref[c0:c1, :] = stage_d_ref[slot].astype(BF16)
                g = _dot(xb, wgu_ref[:, c0:c1])
                u = _dot(xb, wgu_ref[:, d_ff + c0:d_ff + c1])
                a = (g * _sigmoid(g) * u).astype(BF16)
                part = _dot(a, wd_ref[c0:c1, :])
                acc = part if acc is None else acc + part
                if k == 0 and pending is not None:
                    pending()
                    pending = None

            def finish(rows=rows, x=x, acc=acc):
                y = ALPHA * x + 0.5 * acc
                o_ref[rows, :] = _layer_norm_rows(y, lnw_ref[...], lnb_ref[...])

            pending = finish
        pending()

    @pl.when(i == 0)
    def _first():
        run(True)

    @pl.when(i > 0)
    def _rest():
        run(False)


def _ffn_tiles(n_tokens, d_ff):
    tsub = 512 if n_tokens % 512 == 0 else 128
    tm = 2 * tsub if n_tokens % (2 * tsub) == 0 else tsub
    step = 2 * LANES
    assert d_ff % step == 0
    chunks = tuple((c, c + step) for c in range(0, d_ff, step))
    return tm, tsub, chunks


def _ffn_layer(x2d, layer, w_gu, w_down, ln_w, ln_b, ln_idx):
    n, d = x2d.shape
    d_ff = w_down.shape[1]
    tm, tsub, chunks = _ffn_tiles(n, d_ff)
    step = chunks[0][1] - chunks[0][0]

    def const(shape, idx):
        return pl.BlockSpec(shape, lambda i: idx, pipeline_mode=pl.Buffered(1))

    return pl.pallas_call(
        functools.partial(_ffn_kernel, chunks=chunks, layer=layer, tsub=tsub),
        out_shape=jax.ShapeDtypeStruct((n, d), F32),
        grid=(n // tm,),
        in_specs=[
            pl.BlockSpec((tm, d), lambda i: (i, 0)),
            pl.BlockSpec(memory_space=pl.ANY),
            pl.BlockSpec(memory_space=pl.ANY),
            const((None, None, 1, d), (layer, ln_idx, 0, 0)),
            const((None, None, 1, d), (layer, ln_idx, 0, 0)),
        ],
        out_specs=pl.BlockSpec((tm, d), lambda i: (i, 0)),
        scratch_shapes=[
            pltpu.VMEM((d, 2 * d_ff), BF16),
            pltpu.VMEM((d_ff, d), BF16),
            pltpu.VMEM((2, 2, d, step), F32),
            pltpu.VMEM((2, step, d), F32),
            pltpu.SemaphoreType.DMA((2, 3)),
        ],
        compiler_params=pltpu.CompilerParams(
            dimension_semantics=("arbitrary",),
            vmem_limit_bytes=VMEM_LIMIT_BYTES),
        name="swiglu_ln",
    )(x2d, w_gu, w_down, ln_w, ln_b)


def _rope_tables(pos_row, freq, expand):
    ang = freq * pos_row.astype(F32)
    pieces = []
    for t in (jnp.cos(ang), jnp.sin(ang)):
        hi = t.astype(BF16)
        r1 = t - hi.astype(F32)
        mid = r1.astype(BF16)
        lo = (r1 - mid.astype(F32)).astype(BF16)
        pieces += [hi, mid, lo]
    pieces += [jnp.ones((SUBLANES, LANES), BF16), jnp.zeros((SUBLANES, LANES), BF16)]
    cs = _dot_tn(jnp.concatenate(pieces, axis=0), expand)
    return cs[:, 0:LANES], cs[:, LANES:]


def _rope(t, cos_f, sin_f):
    half = ROT_DIM // 2
    lane = lax.broadcasted_iota(jnp.int32, (1, LANES), 1)
    up = pltpu.roll(t, LANES - half, axis=1)
    down = pltpu.roll(t, half, axis=1)
    return t * cos_f + jnp.where((lane & (HEAD_DIM - 1)) < half, up, down) * sin_f


def _split_kv(prev, cur):
    lane = lax.broadcasted_iota(jnp.int32, (1, LANES), 1)
    half0 = lane < HEAD_DIM
    zero = jnp.zeros((), BF16)
    both = jnp.concatenate([prev, cur], axis=0)
    return jnp.concatenate([jnp.where(half0, both, zero),
                            jnp.where(half0, zero, both)], axis=0)


def _attention_scores(q_all, k_cur, k_prev):
    return _dot_nt(q_all, _split_kv(k_prev, k_cur))


def _attention_softmax(s_all, thr, sinks_ref):
    L = ATTN_BLOCK
    lane = lax.broadcasted_iota(jnp.int32, (1, LANES), 1)
    half0 = lane < HEAD_DIM
    qi = lax.broadcasted_iota(jnp.int32, (L, 2 * L), 0)
    kj = lax.broadcasted_iota(jnp.int32, (L, 2 * L), 1)
    rel = qi + L - kj
    mask = (rel >= 0) & (rel < WINDOW) & (kj >= thr)
    ps, scales = [], []
    for c in range(N_Q_HEADS // 2):
        pg, inv = [], []
        for g in range(N_KV_HEADS):
            sink = sinks_ref[Q_HEAD_ORDER[2 * c + g]]
            sg = jnp.where(mask, s_all[c * L:(c + 1) * L, g * 2 * L:(g + 1) * 2 * L], NEG_INF)
            m = jnp.maximum(jnp.max(sg, axis=-1, keepdims=True), sink)
            p = jnp.exp(sg - m)
            den = jnp.sum(p, axis=-1, keepdims=True) + jnp.exp(sink - m)
            pg.append(p.astype(BF16))
            inv.append(1.0 / den)
        ps.append(jnp.concatenate(pg, axis=1))
        scales.append(jnp.where(half0, inv[0], inv[1]))
    return jnp.concatenate(ps, axis=0), jnp.concatenate(scales, axis=0)


def _mlstm_gates(gif, row0):
    L = MLSTM_CHUNK
    rows = slice(row0, row0 + L)
    lane = lax.broadcasted_iota(jnp.int32, (1, LANES), 1)
    gi = gif[rows, 0:LANES]
    gf = gif[rows, LANES:2 * LANES]
    lf = jnp.where(lane < 2 * MLSTM_HEADS, _log_sigmoid(gf), 0.0)
    r_i = lax.broadcasted_iota(jnp.int32, (L, L), 0)
    c_i = lax.broadcasted_iota(jnp.int32, (L, L), 1)
    tril = jnp.where(r_i >= c_i, 1.0, 0.0).astype(BF16)
    hi = lf.astype(BF16)
    r1 = lf - hi.astype(F32)
    mid = r1.astype(BF16)
    lo = (r1 - mid.astype(F32)).astype(BF16)
    bsum = _dot(tril, jnp.concatenate([hi, mid, lo], axis=1))
    bsum = bsum[:, 0:LANES] + bsum[:, LANES:2 * LANES] + bsum[:, 2 * LANES:]
    t_col = jnp.where(lane < MLSTM_HEADS, gi - bsum, bsum)
    return t_col, t_col.T


def _mlstm_stage1(mqk, t_col, t_row, s_ref, m_ref, row0):
    L = MLSTM_CHUNK
    dh = MLSTM_HEAD_DIM
    rows = slice(row0, row0 + L)
    r_i = lax.broadcasted_iota(jnp.int32, (L, L), 0)
    c_i = lax.broadcasted_iota(jnp.int32, (L, L), 1)
    causal = r_i >= c_i
    scale = dh ** -0.5
    heads = []
    for h in range(MLSTM_HEADS):
        q = mqk[rows, h * dh:(h + 1) * dh].astype(BF16)
        k = mqk[rows, MLSTM_WIDTH + h * dh:MLSTM_WIDTH + (h + 1) * dh] * scale
        state = s_ref[h]
        qk = _dot_nt(q, k.astype(BF16))
        from_state = _dot(q, state.astype(BF16))
        heads.append(dict(k=k, state=state, qk=qk, from_state=from_state))
    for h, hd in enumerate(heads):
        b_col = t_col[:, MLSTM_HEADS + h:MLSTM_HEADS + h + 1]
        d_col = t_col[:, h:h + 1]
        d_row = t_row[h:h + 1, :]
        m_prev = m_ref[h][0:1, 0:1]
        logw = jnp.where(causal, b_col + d_row, NEG_INF)
        inter = b_col + m_prev
        m_t = jnp.maximum(inter, jnp.max(logw, axis=-1, keepdims=True))
        scores = hd["qk"] * jnp.exp(logw - m_t)
        m_new = m_t[L - 1:L, :]
        b_last = b_col[L - 1:L, :]
        ws = jnp.exp(b_last + d_col - m_new)
        hd.update(m_t=m_t, w_inter=jnp.exp(inter - m_t), scores=scores,
                  row_sum=jnp.sum(scores, axis=-1, keepdims=True),
                  decay=jnp.exp(b_last + m_prev - m_new), m_new=m_new,
                  kw=(hd["k"] * ws).astype(BF16))
    return heads


def _mlstm_stage2(mvo, heads, s_ref, m_ref, nw_ref, row0):
    L = MLSTM_CHUNK
    dh = MLSTM_HEAD_DIM
    rows = slice(row0, row0 + L)
    lane = lax.broadcasted_iota(jnp.int32, (1, LANES), 1)
    ones_tile = jnp.broadcast_to(jnp.where(lane == 0, 1.0, 0.0).astype(BF16), (L, LANES))
    outs = []
    for h, hd in enumerate(heads):
        v = mvo[rows, h * dh:(h + 1) * dh].astype(BF16)
        hd["intra"] = _dot(hd["scores"].astype(BF16), v)
        v_ext = jnp.concatenate([v, ones_tile], axis=1)
        s_ref[h] = hd["decay"] * hd["state"] + _dot_tn(hd["kw"], v_ext)
        m_ref[h] = jnp.broadcast_to(hd["m_new"], (SUBLANES, LANES))
    for h, hd in enumerate(heads):
        og = mvo[rows, MLSTM_WIDTH + h * dh:MLSTM_WIDTH + (h + 1) * dh]
        num = hd["w_inter"] * hd["from_state"][:, 0:dh] + hd["intra"]
        den = hd["w_inter"] * hd["from_state"][:, dh:dh + 1] + hd["row_sum"]
        hh = num / jnp.maximum(jnp.abs(den), jnp.exp(-hd["m_t"]))
        mu = jnp.mean(hh, axis=-1, keepdims=True)
        hc = hh - mu
        hn = hc * lax.rsqrt(jnp.mean(hc * hc, axis=-1, keepdims=True) + LN_EPS)
        outs.append(hn * nw_ref[:, h * dh:(h + 1) * dh] * _sigmoid(og))
    return outs


class _SubTile:
    def __init__(self, sub, tsub, s_idx, refs):
        self.sub, self.tsub, self.s_idx, self.r = sub, tsub, s_idx, refs
        self.r0 = sub * tsub
        self.nblk = tsub // ATTN_BLOCK
        self.x = refs["x"][0, self.r0:self.r0 + tsub, :]
        self.xb = self.x.astype(BF16)
        self.gates = [[None] * (D_MODEL // GATE_PART) for _ in range(N_BRANCH)]
        self.merged = None

    def proj(self, lo, hi):
        return _dot(self.xb, self.r["win"][:, lo:hi]) + self.r["bin"][:, lo:hi]

    def ropes(self):
        r, n = self.r, self.nblk
        self.rope = [_rope_tables(r["pos"][0, self.sub * n + j:self.sub * n + j + 1, :],
                                  r["freq"][...], r["expand"][...]) for j in range(n)]

    def proj_attn(self):
        self.qkv = self.proj(OFF_AQ, OFF_M)

    def proj_gates(self):
        gif = self.proj(OFF_IF, P_PACKED)
        self.mgates = [_mlstm_gates(gif, j * MLSTM_CHUNK) for j in range(self.nblk)]

    def proj_mlstm_qk(self):
        self.mqk = self.proj(OFF_M, OFF_M + 2 * MLSTM_WIDTH)

    def proj_mlstm_vo(self):
        self.mvo = self.proj(OFF_M + 2 * MLSTM_WIDTH, OFF_C)

    def conv(self):
        r, r0, tsub = self.r, self.r0, self.tsub
        cproj = self.proj(OFF_C, OFF_G)
        u = cproj[:, 2 * CONV_WIDTH:] * cproj[:, 0:CONV_WIDTH]
        r["conv"][SUBLANES + r0:SUBLANES + r0 + tsub, :] = u
        u1 = r["conv"][pl.ds(SUBLANES + r0 - 1, tsub), :]
        u2 = r["conv"][pl.ds(SUBLANES + r0 - 2, tsub), :]
        yc = r["cw"][0:1, :] * u2 + r["cw"][1:2, :] * u1 + r["cw"][2:3, :] * u
        r["y"][r0:r0 + tsub, 2 * BRANCH_WIDTH:] = (cproj[:, CONV_WIDTH:2 * CONV_WIDTH] * yc).astype(BF16)

    def gate(self, b, part):
        lo = OFF_G + b * D_MODEL + part * GATE_PART
        self.gates[b][part] = _sigmoid(self.proj(lo, lo + GATE_PART))

    def stage1(self, j, kv):
        L = ATTN_BLOCK
        rows = slice(j * L, (j + 1) * L)
        cos_f, sin_f = self.rope[j]
        q_all = jnp.concatenate(
            [(_rope(self.qkv[rows, c * LANES:(c + 1) * LANES], cos_f, sin_f)
              * (HEAD_DIM ** -0.5)).astype(BF16) for c in range(N_Q_HEADS // 2)], axis=0)
        self.k_cur = _rope(self.qkv[rows, OFF_AK:OFF_AK + LANES], cos_f, sin_f).astype(BF16)
        self.v_cur = self.qkv[rows, OFF_AV:OFF_AV + LANES].astype(BF16)
        thr = jnp.where(self.s_idx == 0, L, 0) if (self.sub == 0 and j == 0) else 0
        s_all = _attention_scores(q_all, self.k_cur, kv[0])
        self.heads = _mlstm_stage1(self.mqk, self.mgates[j][0], self.mgates[j][1],
                                   self.r["s"], self.r["m"], j * L)
        self.p_all, self.o_scale = _attention_softmax(s_all, thr, self.r["sinks"])

    def stage2(self, j, kv):
        L = ATTN_BLOCK
        r = self.r
        yrows = slice(self.r0 + j * L, self.r0 + (j + 1) * L)
        o_all = _dot(self.p_all, _split_kv(kv[1], self.v_cur)) * self.o_scale
        for c in range(N_Q_HEADS // 2):
            r["y"][yrows, c * LANES:(c + 1) * LANES] = o_all[c * L:(c + 1) * L, :].astype(BF16)
        outs = _mlstm_stage2(self.mvo, self.heads, r["s"], r["m"], r["nw"], j * L)
        for h, o in enumerate(outs):
            r["y"][yrows, BRANCH_WIDTH + h * MLSTM_HEAD_DIM:
                   BRANCH_WIDTH + (h + 1) * MLSTM_HEAD_DIM] = o.astype(BF16)
        return self.k_cur, self.v_cur

    def branch(self, b):
        r, r0, tsub = self.r, self.r0, self.tsub
        term = jnp.concatenate(self.gates[b], axis=1) * _dot(
            r["y"][r0:r0 + tsub, b * BRANCH_WIDTH:(b + 1) * BRANCH_WIDTH], r["wb"][b])
        self.merged = term if self.merged is None else self.merged + term

    def finish(self):
        r = self.r
        mix = _dot(self.merged.astype(BF16), r["wo"][...])
        y = ALPHA * self.x + mix
        r["o"][0, self.r0:self.r0 + self.tsub, :] = _layer_norm_rows(y, r["lnw"][...], r["lnb"][...])


def _mixer_kernel(sinks_ref, x_ref, pos_ref, freq_ref, expand_ref, win_ref, bin_ref, nw_ref, cw_ref,
                  wb_ref, wo_ref, lnw_ref, lnb_ref, o_ref,
                  kprev_ref, vprev_ref, s_ref, m_ref, conv_ref, y_ref, *, tm, tsub):
    s_idx = pl.program_id(1)

    @pl.when(s_idx == 0)
    def _reset():
        kprev_ref[...] = jnp.zeros_like(kprev_ref)
        vprev_ref[...] = jnp.zeros_like(vprev_ref)
        s_ref[...] = jnp.zeros_like(s_ref)
        m_ref[...] = jnp.zeros_like(m_ref)
        conv_ref[0:SUBLANES, :] = jnp.zeros((SUBLANES, CONV_WIDTH), F32)

    refs = dict(sinks=sinks_ref, x=x_ref, pos=pos_ref, freq=freq_ref, expand=expand_ref, win=win_ref,
                bin=bin_ref, nw=nw_ref, cw=cw_ref, wb=wb_ref, wo=wo_ref, lnw=lnw_ref, lnb=lnb_ref,
                o=o_ref, s=s_ref, m=m_ref, conv=conv_ref, y=y_ref)
    nsub = tm // tsub
    tiles = [_SubTile(t, tsub, s_idx, refs) for t in range(nsub)]

    def head_steps(t):
        return [t.ropes, t.proj_attn, t.proj_gates, t.proj_mlstm_qk, t.proj_mlstm_vo]

    def tail_steps(t):
        return [lambda: t.branch(2), lambda: t.branch(0), lambda: t.branch(1), t.finish]

    for step in head_steps(tiles[0]):
        step()
    kv = (kprev_ref[...], vprev_ref[...])
    carry_tail = []
    for i, t in enumerate(tiles):
        own = [t.conv] + [functools.partial(t.gate, b, p) for b in range(N_BRANCH)
                          for p in range(D_MODEL // GATE_PART)]
        nxt = head_steps(tiles[i + 1]) if i + 1 < nsub else []
        fillers = carry_tail + own + nxt
        slots = 2 * t.nblk
        for slot in range(slots):
            j = slot // 2
            if slot % 2 == 0:
                t.stage1(j, kv)
            else:
                kv = t.stage2(j, kv)
            take = -(-len(fillers) // (slots - slot))
            for step in fillers[:take]:
                step()
            fillers = fillers[take:]
        carry_tail = tail_steps(t)
    for step in carry_tail:
        step()
    kprev_ref[...], vprev_ref[...] = kv
    conv_ref[0:SUBLANES, :] = conv_ref[tm:tm + SUBLANES, :]


ROPE_K = 8 * SUBLANES


def _rope_constants():
    half = ROT_DIM // 2
    inv_freq = ROPE_THETA ** (-np.arange(0, ROT_DIM, 2, dtype=np.float64) / ROT_DIM)
    freq = np.repeat(inv_freq[:, None], LANES, axis=1).astype(np.float32)
    l64 = np.arange(LANES) % HEAD_DIM
    expand = np.zeros((ROPE_K, 2 * LANES), np.float32)
    for f in range(half):
        cos_lanes = np.where((l64 < ROT_DIM) & (l64 % half == f), 1.0, 0.0)
        sin_lanes = np.where(l64 == f, -1.0, 0.0) + np.where(l64 == f + half, 1.0, 0.0)
        for piece in range(3):
            expand[piece * SUBLANES + f, :LANES] = cos_lanes
            expand[(3 + piece) * SUBLANES + f, LANES:] = sin_lanes
    expand[6 * SUBLANES, :LANES] = np.where(l64 >= ROT_DIM, 1.0, 0.0)
    return jnp.asarray(freq), jnp.asarray(expand, dtype=BF16)


def _pack_in_proj(w):
    offs = np.concatenate([[0], np.cumsum(SIZES)]).tolist()
    seg = [w[..., offs[i]:offs[i + 1]] for i in range(len(SIZES))]
    (aq, ak, av, mq, mk, mv, mo, mi, mf, cx, cb, cc, gates) = seg
    aq_heads = [aq[..., h * HEAD_DIM:(h + 1) * HEAD_DIM] for h in Q_HEAD_ORDER]
    zi = jnp.zeros(w.shape[:-1] + (LANES - MLSTM_HEADS,), w.dtype)
    zf = jnp.zeros(w.shape[:-1] + (LANES - 2 * MLSTM_HEADS,), w.dtype)
    return jnp.concatenate(aq_heads + [ak, av, mq, mk, mv, mo, cx, cb, cc, gates,
                                       mi, zi, mf, mf, zf], axis=-1)


def _pack_kernel(wt_ref, o_ref):
    offs = np.concatenate([[0], np.cumsum(SIZES)]).tolist()
    lane = lax.broadcasted_iota(jnp.int32, (1, LANES), 1)

    def put(lo, src):
        o_ref[:, lo:lo + src.shape[0]] = src.T.astype(BF16)

    for c in range(N_Q_HEADS // 2):
        pair = [wt_ref[h * HEAD_DIM:(h + 1) * HEAD_DIM, :] for h in Q_HEAD_ORDER[2 * c:2 * c + 2]]
        put(c * LANES, jnp.concatenate(pair, axis=0))
    for lo in range(OFF_AK, OFF_C, 4 * LANES):
        hi = min(lo + 4 * LANES, OFF_C)
        put(lo, wt_ref[lo:hi, :])
    shift = offs[9] - OFF_C
    for lo in range(OFF_C, OFF_IF, 4 * LANES):
        hi = min(lo + 4 * LANES, OFF_IF)
        put(lo, wt_ref[lo + shift:hi + shift, :])
    t = wt_ref[offs[7]:offs[7] + LANES, :].T
    o_ref[:, OFF_IF:OFF_IF + LANES] = jnp.where(lane < MLSTM_HEADS, t, 0.0).astype(BF16)
    f_lo = pltpu.roll(t, LANES - MLSTM_HEADS, axis=1)
    f_tile = jnp.where(lane < MLSTM_HEADS, f_lo, jnp.where(lane < 2 * MLSTM_HEADS, t, 0.0))
    o_ref[:, OFF_IF + LANES:P_PACKED] = f_tile.astype(BF16)


def _pack_in_proj_weights(w_in):
    depth, d, p_in = w_in.shape
    rows = 256
    return pl.pallas_call(
        _pack_kernel,
        out_shape=jax.ShapeDtypeStruct((depth, d, P_PACKED), BF16),
        grid=(depth, d // rows),
        in_specs=[pl.BlockSpec((None, p_in, rows), lambda l, i: (l, 0, i))],
        out_specs=pl.BlockSpec((None, rows, P_PACKED), lambda l, i: (l, i, 0)),
        compiler_params=pltpu.CompilerParams(
            dimension_semantics=("arbitrary", "arbitrary"),
            vmem_limit_bytes=VMEM_LIMIT_BYTES),
        name="pack_in_proj",
    )(jnp.swapaxes(w_in, 1, 2))


def _pack_branch(w_branch):
    wb0 = jnp.concatenate([w_branch[:, 0:1, h * HEAD_DIM:(h + 1) * HEAD_DIM, :] for h in Q_HEAD_ORDER],
                          axis=2)
    return jnp.concatenate([wb0, w_branch[:, 1:]], axis=1)


def _mixer_tile(seq):
    if seq % 512 == 0:
        return 512, 256
    if seq % 256 == 0:
        return 256, 256
    return ATTN_BLOCK, ATTN_BLOCK


def _mixer_layer(x, layer, positions, freq, expand, win, bin_, sinks, norm_w, conv_w, wb, wo, ln_w, ln_b):
    bsz, seq, d = x.shape
    tm, tsub = _mixer_tile(seq)
    ns = seq // tm
    nblk = tm // ATTN_BLOCK
    pos = positions.reshape(bsz * ns, nblk, ATTN_BLOCK)

    def const(shape, idx):
        return pl.BlockSpec(shape, lambda b, s, *_: idx, pipeline_mode=pl.Buffered(1))

    grid_spec = pltpu.PrefetchScalarGridSpec(
        num_scalar_prefetch=1,
        grid=(bsz, ns),
        in_specs=[
            pl.BlockSpec((1, tm, d), lambda b, s, *_: (b, s, 0)),
            pl.BlockSpec((1, nblk, ATTN_BLOCK), lambda b, s, *_: (b * ns + s, 0, 0)),
            const((SUBLANES, LANES), (0, 0)),
            const((ROPE_K, 2 * LANES), (0, 0)),
            const((None, d, P_PACKED), (layer, 0, 0)),
            const((None, 1, P_PACKED), (layer, 0, 0)),
            const((None, 1, MLSTM_WIDTH), (layer, 0, 0)),
            const((None, CONV_K, CONV_WIDTH), (layer, 0, 0)),
            const((None, N_BRANCH, BRANCH_WIDTH, d), (layer, 0, 0, 0)),
            const((None, d, d), (layer, 0, 0)),
            const((None, None, 1, d), (layer, 1, 0, 0)),
            const((None, None, 1, d), (layer, 1, 0, 0)),
        ],
        out_specs=pl.BlockSpec((1, tm, d), lambda b, s, *_: (b, s, 0)),
        scratch_shapes=[
            pltpu.VMEM((ATTN_BLOCK, LANES), BF16),
            pltpu.VMEM((ATTN_BLOCK, LANES), BF16),
            pltpu.VMEM((MLSTM_HEADS, MLSTM_HEAD_DIM, 2 * MLSTM_HEAD_DIM), F32),
            pltpu.VMEM((MLSTM_HEADS, SUBLANES, LANES), F32),
            pltpu.VMEM((tm + 2 * SUBLANES, CONV_WIDTH), F32),
            pltpu.VMEM((tm, N_BRANCH * BRANCH_WIDTH), BF16),
        ],
    )
    return pl.pallas_call(
        functools.partial(_mixer_kernel, tm=tm, tsub=tsub),
        out_shape=jax.ShapeDtypeStruct((bsz, seq, d), F32),
        grid_spec=grid_spec,
        compiler_params=pltpu.CompilerParams(
            dimension_semantics=("arbitrary", "arbitrary"),
            vmem_limit_bytes=VMEM_LIMIT_BYTES),
        name="token_mixer",
    )(sinks, x, pos, freq, expand, win, bin_, norm_w, conv_w, wb, wo, ln_w, ln_b)


def kernel(x, positions, w_in, b_in, attn_sinks, mlstm_norm_w, conv_w, w_branch, w_out,
           ffn1_w_gu, ffn1_w_down, ffn2_w_gu, ffn2_w_down, ln_w, ln_b):
    bsz, seq, d = x.shape
    depth = w_in.shape[0]
    ffn_w = ((ffn1_w_gu, ffn1_w_down), (ffn2_w_gu, ffn2_w_down))
    win = _pack_in_proj_weights(w_in)
    bin_ = _pack_in_proj(b_in)[:, None, :]
    wb = _pack_branch(w_branch.astype(BF16))
    wo = w_out.astype(BF16)
    norm_w = mlstm_norm_w[:, None, :]
    ln_w4 = ln_w[:, :, None, :]
    ln_b4 = ln_b[:, :, None, :]
    freq, expand = _rope_constants()
    for l in range(depth):
        x = _ffn_layer(x.reshape(bsz * seq, d), l, ffn_w[0][0], ffn_w[0][1], ln_w4, ln_b4, 0
                       ).reshape(bsz, seq, d)
        x = _mixer_layer(x, l, positions, freq, expand, win, bin_, attn_sinks[l], norm_w, conv_w, wb, wo,
                         ln_w4, ln_b4)
        x = _ffn_layer(x.reshape(bsz * seq, d), l, ffn_w[1][0], ffn_w[1][1], ln_w4, ln_b4, 2
                       ).reshape(bsz, seq, d)
    return x
```

```python
import functools
import math

import numpy as np
import jax
import jax.numpy as jnp
from jax import lax
from jax.experimental import pallas as pl
from jax.experimental.pallas import tpu as pltpu

D_MODEL = 1024
DEPTH = 2

HEAD_DIM = 64
N_Q_HEADS = 8
N_KV_HEADS = 2
GROUP = N_Q_HEADS // N_KV_HEADS
WINDOW = 128
ATTN_BLOCK = 128
ROPE_THETA = 500000.0
ROT_DIM = HEAD_DIM // 4
ATTN_WIDTH = N_Q_HEADS * HEAD_DIM
KV_WIDTH = N_KV_HEADS * HEAD_DIM

MLSTM_HEADS = 4
MLSTM_HEAD_DIM = D_MODEL // 8
MLSTM_WIDTH = MLSTM_HEADS * MLSTM_HEAD_DIM
MLSTM_CHUNK = 128

CONV_WIDTH = D_MODEL // 2
CONV_K = 3

N_BRANCH = 3
BRANCH_WIDTH = 512
D_FF = 2816
LN_EPS = 1e-5
ALPHA = (2.0 * DEPTH) ** 0.25
NEG_INF = -1e30

SIZES = (ATTN_WIDTH, KV_WIDTH, KV_WIDTH,
         MLSTM_WIDTH, MLSTM_WIDTH, MLSTM_WIDTH, MLSTM_WIDTH, MLSTM_HEADS, MLSTM_HEADS,
         CONV_WIDTH, CONV_WIDTH, CONV_WIDTH,
         N_BRANCH * D_MODEL)

LANES = 128
SUBLANES = 8
VMEM_LIMIT_BYTES = 56 * 1024 * 1024

OFF_AQ = 0
OFF_AK = OFF_AQ + ATTN_WIDTH
OFF_AV = OFF_AK + KV_WIDTH
OFF_M = OFF_AV + KV_WIDTH
OFF_C = OFF_M + 4 * MLSTM_WIDTH
OFF_G = OFF_C + 3 * CONV_WIDTH
OFF_IF = OFF_G + N_BRANCH * D_MODEL
P_PACKED = OFF_IF + 2 * LANES
GATE_PART = D_MODEL

Q_HEAD_ORDER = (0, 4, 1, 5, 2, 6, 3, 7)

BF16 = jnp.bfloat16
F32 = jnp.float32


def _dot(a, b):
    return jnp.dot(a, b, preferred_element_type=F32)


def _dot_nt(a, b):
    return lax.dot_general(a, b, (((1,), (1,)), ((), ())), preferred_element_type=F32)


def _dot_tn(a, b):
    return lax.dot_general(a, b, (((0,), (0,)), ((), ())), preferred_element_type=F32)


def _layer_norm_rows(y, w, b):
    mu = jnp.mean(y, axis=-1, keepdims=True)
    yc = y - mu
    var = jnp.mean(yc * yc, axis=-1, keepdims=True)
    return yc * lax.rsqrt(var + LN_EPS) * w + b


def _sigmoid(x):
    return 0.5 * jnp.tanh(0.5 * x) + 0.5


def _log_sigmoid(x):
    return jnp.minimum(x, 0.0) - jnp.log1p(jnp.exp(-jnp.abs(x)))


def _ffn_kernel(x_ref, wgu_hbm, wd_hbm, lnw_ref, lnb_ref, o_ref,
                wgu_ref, wd_ref, stage_gu_ref, stage_d_ref, sem, *, chunks, layer):
    i = pl.program_id(0)
    d_ff = wd_ref.shape[0]

    def chunk_copies(k, slot):
        c0, c1 = chunks[k]
        return (
            pltpu.make_async_copy(wgu_hbm.at[layer, :, c0:c1], stage_gu_ref.at[slot, 0], sem.at[slot, 0]),
            pltpu.make_async_copy(wgu_hbm.at[layer, :, d_ff + c0:d_ff + c1], stage_gu_ref.at[slot, 1],
                                  sem.at[slot, 1]),
            pltpu.make_async_copy(wd_hbm.at[layer, c0:c1, :], stage_d_ref.at[slot], sem.at[slot, 2]),
        )

    def run(load_weights):
        x = x_ref[...]
        xb = x.astype(BF16)
        acc = None
        if load_weights:
            for cp in chunk_copies(0, 0):
                cp.start()
        for k, (c0, c1) in enumerate(chunks):
            if load_weights:
                slot = k % 2
                if k + 1 < len(chunks):
                    for cp in chunk_copies(k + 1, 1 - slot):
                        cp.start()
                for cp in chunk_copies(k, slot):
                    cp.wait()
                wgu_ref[:, c0:c1] = stage_gu_ref[slot, 0].astype(BF16)
                wgu_ref[:, d_ff + c0:d_ff + c1] = stage_gu_ref[slot, 1].astype(BF16)
                wd_ref[c0:c1, :] = stage_d_ref[slot].astype(BF16)
            g = _dot(xb, wgu_ref[:, c0:c1])
            u = _dot(xb, wgu_ref[:, d_ff + c0:d_ff + c1])
            a = (g * _sigmoid(g) * u).astype(BF16)
            part = _dot(a, wd_ref[c0:c1, :])
            acc = part if acc is None else acc + part
        y = ALPHA * x + 0.5 * acc
        o_ref[...] = _layer_norm_rows(y, lnw_ref[...], lnb_ref[...])

    @pl.when(i == 0)
    def _first():
        run(True)

    @pl.when(i > 0)
    def _rest():
        run(False)


def _ffn_tiles(n_tokens, d_ff):
    tm = 512 if n_tokens % 512 == 0 else 128
    step = 2 * LANES
    assert d_ff % step == 0
    chunks = tuple((c, c + step) for c in range(0, d_ff, step))
    return tm, chunks


def _ffn_layer(x2d, layer, w_gu, w_down, ln_w, ln_b, ln_idx):
    n, d = x2d.shape
    d_ff = w_down.shape[1]
    tm, chunks = _ffn_tiles(n, d_ff)
    step = chunks[0][1] - chunks[0][0]

    def const(shape, idx):
        return pl.BlockSpec(shape, lambda i: idx, pipeline_mode=pl.Buffered(1))

    return pl.pallas_call(
        functools.partial(_ffn_kernel, chunks=chunks, layer=layer),
        out_shape=jax.ShapeDtypeStruct((n, d), F32),
        grid=(n // tm,),
        in_specs=[
            pl.BlockSpec((tm, d), lambda i: (i, 0)),
            pl.BlockSpec(memory_space=pl.ANY),
            pl.BlockSpec(memory_space=pl.ANY),
            const((None, None, 1, d), (layer, ln_idx, 0, 0)),
            const((None, None, 1, d), (layer, ln_idx, 0, 0)),
        ],
        out_specs=pl.BlockSpec((tm, d), lambda i: (i, 0)),
        scratch_shapes=[
            pltpu.VMEM((d, 2 * d_ff), BF16),
            pltpu.VMEM((d_ff, d), BF16),
            pltpu.VMEM((2, 2, d, step), F32),
            pltpu.VMEM((2, step, d), F32),
            pltpu.SemaphoreType.DMA((2, 3)),
        ],
        compiler_params=pltpu.CompilerParams(
            dimension_semantics=("arbitrary",),
            vmem_limit_bytes=VMEM_LIMIT_BYTES),
        name="swiglu_ln",
    )(x2d, w_gu, w_down, ln_w, ln_b)


def _rope_tables(pos_row, freq, expand):
    ang = freq * pos_row.astype(F32)
    pieces = []
    for t in (jnp.cos(ang), jnp.sin(ang)):
        hi = t.astype(BF16)
        r1 = t - hi.astype(F32)
        mid = r1.astype(BF16)
        lo = (r1 - mid.astype(F32)).astype(BF16)
        pieces += [hi, mid, lo]
    pieces += [jnp.ones((SUBLANES, LANES), BF16), jnp.zeros((SUBLANES, LANES), BF16)]
    cs = _dot_tn(jnp.concatenate(pieces, axis=0), expand)
    return cs[:, 0:LANES], cs[:, LANES:]


def _rope(t, cos_f, sin_f):
    half = ROT_DIM // 2
    lane = lax.broadcasted_iota(jnp.int32, (1, LANES), 1)
    up = pltpu.roll(t, LANES - half, axis=1)
    down = pltpu.roll(t, half, axis=1)
    return t * cos_f + jnp.where((lane & (HEAD_DIM - 1)) < half, up, down) * sin_f


def _split_kv(prev, cur):
    lane = lax.broadcasted_iota(jnp.int32, (1, LANES), 1)
    half0 = lane < HEAD_DIM
    zero = jnp.zeros((), BF16)
    both = jnp.concatenate([prev, cur], axis=0)
    return jnp.concatenate([jnp.where(half0, both, zero),
                            jnp.where(half0, zero, both)], axis=0)


def _attention_scores(q_all, k_cur, k_prev):
    return _dot_nt(q_all, _split_kv(k_prev, k_cur))


def _attention_softmax(s_all, thr, sinks_ref):
    L = ATTN_BLOCK
    lane = lax.broadcasted_iota(jnp.int32, (1, LANES), 1)
    half0 = lane < HEAD_DIM
    qi = lax.broadcasted_iota(jnp.int32, (L, 2 * L), 0)
    kj = lax.broadcasted_iota(jnp.int32, (L, 2 * L), 1)
    rel = qi + L - kj
    mask = (rel >= 0) & (rel < WINDOW) & (kj >= thr)
    ps, scales = [], []
    for c in range(N_Q_HEADS // 2):
        pg, inv = [], []
        for g in range(N_KV_HEADS):
            sink = sinks_ref[Q_HEAD_ORDER[2 * c + g]]
            sg = jnp.where(mask, s_all[c * L:(c + 1) * L, g * 2 * L:(g + 1) * 2 * L], NEG_INF)
            m = jnp.maximum(jnp.max(sg, axis=-1, keepdims=True), sink)
            p = jnp.exp(sg - m)
            den = jnp.sum(p, axis=-1, keepdims=True) + jnp.exp(sink - m)
            pg.append(p.astype(BF16))
            inv.append(1.0 / den)
        ps.append(jnp.concatenate(pg, axis=1))
        scales.append(jnp.where(half0, inv[0], inv[1]))
    return jnp.concatenate(ps, axis=0), jnp.concatenate(scales, axis=0)


def _mlstm_gates(gif, row0):
    L = MLSTM_CHUNK
    rows = slice(row0, row0 + L)
    lane = lax.broadcasted_iota(jnp.int32, (1, LANES), 1)
    gi = gif[rows, 0:LANES]
    gf = gif[rows, LANES:2 * LANES]
    lf = jnp.where(lane < 2 * MLSTM_HEADS, _log_sigmoid(gf), 0.0)
    r_i = lax.broadcasted_iota(jnp.int32, (L, L), 0)
    c_i = lax.broadcasted_iota(jnp.int32, (L, L), 1)
    tril = jnp.where(r_i >= c_i, 1.0, 0.0).astype(BF16)
    hi = lf.astype(BF16)
    r1 = lf - hi.astype(F32)
    mid = r1.astype(BF16)
    lo = (r1 - mid.astype(F32)).astype(BF16)
    bsum = _dot(tril, jnp.concatenate([hi, mid, lo], axis=1))
    bsum = bsum[:, 0:LANES] + bsum[:, LANES:2 * LANES] + bsum[:, 2 * LANES:]
    t_col = jnp.where(lane < MLSTM_HEADS, gi - bsum, bsum)
    return t_col, t_col.T


def _mlstm_stage1(mqk, t_col, t_row, s_ref, m_ref, row0):
    L = MLSTM_CHUNK
    dh = MLSTM_HEAD_DIM
    rows = slice(row0, row0 + L)
    r_i = lax.broadcasted_iota(jnp.int32, (L, L), 0)
    c_i = lax.broadcasted_iota(jnp.int32, (L, L), 1)
    causal = r_i >= c_i
    scale = dh ** -0.5
    heads = []
    for h in range(MLSTM_HEADS):
        q = mqk[rows, h * dh:(h + 1) * dh].astype(BF16)
        k = mqk[rows, MLSTM_WIDTH + h * dh:MLSTM_WIDTH + (h + 1) * dh] * scale
        state = s_ref[h]
        qk = _dot_nt(q, k.astype(BF16))
        from_state = _dot(q, state.astype(BF16))
        heads.append(dict(k=k, state=state, qk=qk, from_state=from_state))
    for h, hd in enumerate(heads):
        b_col = t_col[:, MLSTM_HEADS + h:MLSTM_HEADS + h + 1]
        d_col = t_col[:, h:h + 1]
        d_row = t_row[h:h + 1, :]
        m_prev = m_ref[h][0:1, 0:1]
        logw = jnp.where(causal, b_col + d_row, NEG_INF)
        inter = b_col + m_prev
        m_t = jnp.maximum(inter, jnp.max(logw, axis=-1, keepdims=True))
        scores = hd["qk"] * jnp.exp(logw - m_t)
        m_new = m_t[L - 1:L, :]
        b_last = b_col[L - 1:L, :]
        ws = jnp.exp(b_last + d_col - m_new)
        hd.update(m_t=m_t, w_inter=jnp.exp(inter - m_t), scores=scores,
                  row_sum=jnp.sum(scores, axis=-1, keepdims=True),
                  decay=jnp.exp(b_last + m_prev - m_new), m_new=m_new,
                  kw=(hd["k"] * ws).astype(BF16))
    return heads


def _mlstm_stage2(mvo, heads, s_ref, m_ref, nw_ref, row0):
    L = MLSTM_CHUNK
    dh = MLSTM_HEAD_DIM
    rows = slice(row0, row0 + L)
    lane = lax.broadcasted_iota(jnp.int32, (1, LANES), 1)
    ones_tile = jnp.broadcast_to(jnp.where(lane == 0, 1.0, 0.0).astype(BF16), (L, LANES))
    outs = []
    for h, hd in enumerate(heads):
        v = mvo[rows, h * dh:(h + 1) * dh].astype(BF16)
        hd["intra"] = _dot(hd["scores"].astype(BF16), v)
        v_ext = jnp.concatenate([v, ones_tile], axis=1)
        s_ref[h] = hd["decay"] * hd["state"] + _dot_tn(hd["kw"], v_ext)
        m_ref[h] = jnp.broadcast_to(hd["m_new"], (SUBLANES, LANES))
    for h, hd in enumerate(heads):
        og = mvo[rows, MLSTM_WIDTH + h * dh:MLSTM_WIDTH + (h + 1) * dh]
        num = hd["w_inter"] * hd["from_state"][:, 0:dh] + hd["intra"]
        den = hd["w_inter"] * hd["from_state"][:, dh:dh + 1] + hd["row_sum"]
        hh = num / jnp.maximum(jnp.abs(den), jnp.exp(-hd["m_t"]))
        mu = jnp.mean(hh, axis=-1, keepdims=True)
        hc = hh - mu
        hn = hc * lax.rsqrt(jnp.mean(hc * hc, axis=-1, keepdims=True) + LN_EPS)
        outs.append(hn * nw_ref[:, h * dh:(h + 1) * dh] * _sigmoid(og))
    return outs


class _SubTile:
    def __init__(self, sub, tsub, s_idx, refs):
        self.sub, self.tsub, self.s_idx, self.r = sub, tsub, s_idx, refs
        self.r0 = sub * tsub
        self.nblk = tsub // ATTN_BLOCK
        self.x = refs["x"][0, self.r0:self.r0 + tsub, :]
        self.xb = self.x.astype(BF16)
        self.gates = [[None] * (D_MODEL // GATE_PART) for _ in range(N_BRANCH)]
        self.merged = None

    def proj(self, lo, hi):
        return _dot(self.xb, self.r["win"][:, lo:hi]) + self.r["bin"][:, lo:hi]

    def ropes(self):
        r, n = self.r, self.nblk
        self.rope = [_rope_tables(r["pos"][0, self.sub * n + j:self.sub * n + j + 1, :],
                                  r["freq"][...], r["expand"][...]) for j in range(n)]

    def proj_attn(self):
        self.qkv = self.proj(OFF_AQ, OFF_M)

    def proj_gates(self):
        gif = self.proj(OFF_IF, P_PACKED)
        self.mgates = [_mlstm_gates(gif, j * MLSTM_CHUNK) for j in range(self.nblk)]

    def proj_mlstm_qk(self):
        self.mqk = self.proj(OFF_M, OFF_M + 2 * MLSTM_WIDTH)

    def proj_mlstm_vo(self):
        self.mvo = self.proj(OFF_M + 2 * MLSTM_WIDTH, OFF_C)

    def conv(self):
        r, r0, tsub = self.r, self.r0, self.tsub
        cproj = self.proj(OFF_C, OFF_G)
        u = cproj[:, 2 * CONV_WIDTH:] * cproj[:, 0:CONV_WIDTH]
        r["conv"][SUBLANES + r0:SUBLANES + r0 + tsub, :] = u
        u1 = r["conv"][pl.ds(SUBLANES + r0 - 1, tsub), :]
        u2 = r["conv"][pl.ds(SUBLANES + r0 - 2, tsub), :]
        yc = r["cw"][0:1, :] * u2 + r["cw"][1:2, :] * u1 + r["cw"][2:3, :] * u
        r["y"][r0:r0 + tsub, 2 * BRANCH_WIDTH:] = (cproj[:, CONV_WIDTH:2 * CONV_WIDTH] * yc).astype(BF16)

    def gate(self, b, part):
        lo = OFF_G + b * D_MODEL + part * GATE_PART
        self.gates[b][part] = _sigmoid(self.proj(lo, lo + GATE_PART))

    def stage1(self, j, kv):
        L = ATTN_BLOCK
        rows = slice(j * L, (j + 1) * L)
        cos_f, sin_f = self.rope[j]
        q_all = jnp.concatenate(
            [(_rope(self.qkv[rows, c * LANES:(c + 1) * LANES], cos_f, sin_f)
              * (HEAD_DIM ** -0.5)).astype(BF16) for c in range(N_Q_HEADS // 2)], axis=0)
        self.k_cur = _rope(self.qkv[rows, OFF_AK:OFF_AK + LANES], cos_f, sin_f).astype(BF16)
        self.v_cur = self.qkv[rows, OFF_AV:OFF_AV + LANES].astype(BF16)
        thr = jnp.where(self.s_idx == 0, L, 0) if (self.sub == 0 and j == 0) else 0
        s_all = _attention_scores(q_all, self.k_cur, kv[0])
        self.heads = _mlstm_stage1(self.mqk, self.mgates[j][0], self.mgates[j][1],
                                   self.r["s"], self.r["m"], j * L)
        self.p_all, self.o_scale = _attention_softmax(s_all, thr, self.r["sinks"])

    def stage2(self, j, kv):
        L = ATTN_BLOCK
        r = self.r
        yrows = slice(self.r0 + j * L, self.r0 + (j + 1) * L)
        o_all = _dot(self.p_all, _split_kv(kv[1], self.v_cur)) * self.o_scale
        for c in range(N_Q_HEADS // 2):
            r["y"][yrows, c * LANES:(c + 1) * LANES] = o_all[c * L:(c + 1) * L, :].astype(BF16)
        outs = _mlstm_stage2(self.mvo, self.heads, r["s"], r["m"], r["nw"], j * L)
        for h, o in enumerate(outs):
            r["y"][yrows, BRANCH_WIDTH + h * MLSTM_HEAD_DIM:
                   BRANCH_WIDTH + (h + 1) * MLSTM_HEAD_DIM] = o.astype(BF16)
        return self.k_cur, self.v_cur

    def branch(self, b):
        r, r0, tsub = self.r, self.r0, self.tsub
        term = jnp.concatenate(self.gates[b], axis=1) * _dot(
            r["y"][r0:r0 + tsub, b * BRANCH_WIDTH:(b + 1) * BRANCH_WIDTH], r["wb"][b])
        self.merged = term if self.merged is None else self.merged + term

    def finish(self):
        r = self.r
        mix = _dot(self.merged.astype(BF16), r["wo"][...])
        y = ALPHA * self.x + mix
        r["o"][0, self.r0:self.r0 + self.tsub, :] = _layer_norm_rows(y, r["lnw"][...], r["lnb"][...])


def _mixer_kernel(sinks_ref, x_ref, pos_ref, freq_ref, expand_ref, win_ref, bin_ref, nw_ref, cw_ref,
                  wb_ref, wo_ref, lnw_ref, lnb_ref, o_ref,
                  kprev_ref, vprev_ref, s_ref, m_ref, conv_ref, y_ref, *, tm, tsub):
    s_idx = pl.program_id(1)

    @pl.when(s_idx == 0)
    def _reset():
        kprev_ref[...] = jnp.zeros_like(kprev_ref)
        vprev_ref[...] = jnp.zeros_like(vprev_ref)
        s_ref[...] = jnp.zeros_like(s_ref)
        m_ref[...] = jnp.zeros_like(m_ref)
        conv_ref[0:SUBLANES, :] = jnp.zeros((SUBLANES, CONV_WIDTH), F32)

    refs = dict(sinks=sinks_ref, x=x_ref, pos=pos_ref, freq=freq_ref, expand=expand_ref, win=win_ref,
                bin=bin_ref, nw=nw_ref, cw=cw_ref, wb=wb_ref, wo=wo_ref, lnw=lnw_ref, lnb=lnb_ref,
                o=o_ref, s=s_ref, m=m_ref, conv=conv_ref, y=y_ref)
    nsub = tm // tsub
    tiles = [_SubTile(t, tsub, s_idx, refs) for t in range(nsub)]

    def head_steps(t):
        return [t.ropes, t.proj_attn, t.proj_gates, t.proj_mlstm_qk, t.proj_mlstm_vo]

    def tail_steps(t):
        return [lambda: t.branch(2), lambda: t.branch(0), lambda: t.branch(1), t.finish]

    for step in head_steps(tiles[0]):
        step()
    kv = (kprev_ref[...], vprev_ref[...])
    carry_tail = []
    for i, t in enumerate(tiles):
        own = [t.conv] + [functools.partial(t.gate, b, p) for b in range(N_BRANCH)
                          for p in range(D_MODEL // GATE_PART)]
        nxt = head_steps(tiles[i + 1]) if i + 1 < nsub else []
        fillers = carry_tail + own + nxt
        slots = 2 * t.nblk
        for slot in range(slots):
            j = slot // 2
            if slot % 2 == 0:
                t.stage1(j, kv)
            else:
                kv = t.stage2(j, kv)
            take = -(-len(fillers) // (slots - slot))
            for step in fillers[:take]:
                step()
            fillers = fillers[take:]
        carry_tail = tail_steps(t)
    for step in carry_tail:
        step()
    kprev_ref[...], vprev_ref[...] = kv
    conv_ref[0:SUBLANES, :] = conv_ref[tm:tm + SUBLANES, :]


ROPE_K = 8 * SUBLANES


def _rope_constants():
    half = ROT_DIM // 2
    inv_freq = ROPE_THETA ** (-np.arange(0, ROT_DIM, 2, dtype=np.float64) / ROT_DIM)
    freq = np.repeat(inv_freq[:, None], LANES, axis=1).astype(np.float32)
    l64 = np.arange(LANES) % HEAD_DIM
    expand = np.zeros((ROPE_K, 2 * LANES), np.float32)
    for f in range(half):
        cos_lanes = np.where((l64 < ROT_DIM) & (l64 % half == f), 1.0, 0.0)
        sin_lanes = np.where(l64 == f, -1.0, 0.0) + np.where(l64 == f + half, 1.0, 0.0)
        for piece in range(3):
            expand[piece * SUBLANES + f, :LANES] = cos_lanes
            expand[(3 + piece) * SUBLANES + f, LANES:] = sin_lanes
    expand[6 * SUBLANES, :LANES] = np.where(l64 >= ROT_DIM, 1.0, 0.0)
    return jnp.asarray(freq), jnp.asarray(expand, dtype=BF16)


def _pack_in_proj(w):
    offs = np.concatenate([[0], np.cumsum(SIZES)]).tolist()
    seg = [w[..., offs[i]:offs[i + 1]] for i in range(len(SIZES))]
    (aq, ak, av, mq, mk, mv, mo, mi, mf, cx, cb, cc, gates) = seg
    aq_heads = [aq[..., h * HEAD_DIM:(h + 1) * HEAD_DIM] for h in Q_HEAD_ORDER]
    zi = jnp.zeros(w.shape[:-1] + (LANES - MLSTM_HEADS,), w.dtype)
    zf = jnp.zeros(w.shape[:-1] + (LANES - 2 * MLSTM_HEADS,), w.dtype)
    return jnp.concatenate(aq_heads + [ak, av, mq, mk, mv, mo, cx, cb, cc, gates,
                                       mi, zi, mf, mf, zf], axis=-1)


def _pack_kernel(wt_ref, o_ref):
    offs = np.concatenate([[0], np.cumsum(SIZES)]).tolist()
    lane = lax.broadcasted_iota(jnp.int32, (1, LANES), 1)

    def put(lo, src):
        o_ref[:, lo:lo + src.shape[0]] = src.T.astype(BF16)

    for c in range(N_Q_HEADS // 2):
        pair = [wt_ref[h * HEAD_DIM:(h + 1) * HEAD_DIM, :] for h in Q_HEAD_ORDER[2 * c:2 * c + 2]]
        put(c * LANES, jnp.concatenate(pair, axis=0))
    for lo in range(OFF_AK, OFF_C, 4 * LANES):
        hi = min(lo + 4 * LANES, OFF_C)
        put(lo, wt_ref[lo:hi, :])
    shift = offs[9] - OFF_C
    for lo in range(OFF_C, OFF_IF, 4 * LANES):
        hi = min(lo + 4 * LANES, OFF_IF)
        put(lo, wt_ref[lo + shift:hi + shift, :])
    t = wt_ref[offs[7]:offs[7] + LANES, :].T
    o_ref[:, OFF_IF:OFF_IF + LANES] = jnp.where(lane < MLSTM_HEADS, t, 0.0).astype(BF16)
    f_lo = pltpu.roll(t, LANES - MLSTM_HEADS, axis=1)
    f_tile = jnp.where(lane < MLSTM_HEADS, f_lo, jnp.where(lane < 2 * MLSTM_HEADS, t, 0.0))
    o_ref[:, OFF_IF + LANES:P_PACKED] = f_tile.astype(BF16)


def _pack_in_proj_weights(w_in):
    depth, d, p_in = w_in.shape
    rows = 256
    return pl.pallas_call(
        _pack_kernel,
        out_shape=jax.ShapeDtypeStruct((depth, d, P_PACKED), BF16),
        grid=(depth, d // rows),
        in_specs=[pl.BlockSpec((None, p_in, rows), lambda l, i: (l, 0, i))],
        out_specs=pl.BlockSpec((None, rows, P_PACKED), lambda l, i: (l, i, 0)),
        compiler_params=pltpu.CompilerParams(
            dimension_semantics=("arbitrary", "arbitrary"),
            vmem_limit_bytes=VMEM_LIMIT_BYTES),
        name="pack_in_proj",
    )(jnp.swapaxes(w_in, 1, 2))


def _pack_branch(w_branch):
    wb0 = jnp.concatenate([w_branch[:, 0:1, h * HEAD_DIM:(h + 1) * HEAD_DIM, :] for h in Q_HEAD_ORDER],
                          axis=2)
    return jnp.concatenate([wb0, w_branch[:, 1:]], axis=1)


def _mixer_tile(seq):
    if seq % 1024 == 0:
        return 1024, 256
    if seq % 512 == 0:
        return 512, 256
    if seq % 256 == 0:
        return 256, 256
    return ATTN_BLOCK, ATTN_BLOCK


def _mixer_layer(x, layer, positions, freq, expand, win, bin_, sinks, norm_w, conv_w, wb, wo, ln_w, ln_b):
    bsz, seq, d = x.shape
    tm, tsub = _mixer_tile(seq)
    ns = seq // tm
    nblk = tm // ATTN_BLOCK
    pos = positions.reshape(bsz * ns, nblk, ATTN_BLOCK)

    def const(shape, idx):
        return pl.BlockSpec(shape, lambda b, s, *_: idx, pipeline_mode=pl.Buffered(1))

    grid_spec = pltpu.PrefetchScalarGridSpec(
        num_scalar_prefetch=1,
        grid=(bsz, ns),
        in_specs=[
            pl.BlockSpec((1, tm, d), lambda b, s, *_: (b, s, 0)),
            pl.BlockSpec((1, nblk, ATTN_BLOCK), lambda b, s, *_: (b * ns + s, 0, 0)),
            const((SUBLANES, LANES), (0, 0)),
            const((ROPE_K, 2 * LANES), (0, 0)),
            const((None, d, P_PACKED), (layer, 0, 0)),
            const((None, 1, P_PACKED), (layer, 0, 0)),
            const((None, 1, MLSTM_WIDTH), (layer, 0, 0)),
            const((None, CONV_K, CONV_WIDTH), (layer, 0, 0)),
            const((None, N_BRANCH, BRANCH_WIDTH, d), (layer, 0, 0, 0)),
            const((None, d, d), (layer, 0, 0)),
            const((None, None, 1, d), (layer, 1, 0, 0)),
            const((None, None, 1, d), (layer, 1, 0, 0)),
        ],
        out_specs=pl.BlockSpec((1, tm, d), lambda b, s, *_: (b, s, 0)),
        scratch_shapes=[
            pltpu.VMEM((ATTN_BLOCK, LANES), BF16),
            pltpu.VMEM((ATTN_BLOCK, LANES), BF16),
            pltpu.VMEM((MLSTM_HEADS, MLSTM_HEAD_DIM, 2 * MLSTM_HEAD_DIM), F32),
            pltpu.VMEM((MLSTM_HEADS, SUBLANES, LANES), F32),
            pltpu.VMEM((tm + 2 * SUBLANES, CONV_WIDTH), F32),
            pltpu.VMEM((tm, N_BRANCH * BRANCH_WIDTH), BF16),
        ],
    )
    return pl.pallas_call(
        functools.partial(_mixer_kernel, tm=tm, tsub=tsub),
        out_shape=jax.ShapeDtypeStruct((bsz, seq, d), F32),
        grid_spec=grid_spec,
        compiler_params=pltpu.CompilerParams(
            dimension_semantics=("arbitrary", "arbitrary"),
            vmem_limit_bytes=VMEM_LIMIT_BYTES),
        name="token_mixer",
    )(sinks, x, pos, freq, expand, win, bin_, norm_w, conv_w, wb, wo, ln_w, ln_b)


def kernel(x, positions, w_in, b_in, attn_sinks, mlstm_norm_w, conv_w, w_branch, w_out,
           ffn1_w_gu, ffn1_w_down, ffn2_w_gu, ffn2_w_down, ln_w, ln_b):
    bsz, seq, d = x.shape
    depth = w_in.shape[0]
    ffn_w = ((ffn1_w_gu, ffn1_w_down), (ffn2_w_gu, ffn2_w_down))
    win = _pack_in_proj_weights(w_in)
    bin_ = _pack_in_proj(b_in)[:, None, :]
    wb = _pack_branch(w_branch.astype(BF16))
    wo = w_out.astype(BF16)
    norm_w = mlstm_norm_w[:, None, :]
    ln_w4 = ln_w[:, :, None, :]
    ln_b4 = ln_b[:, :, None, :]
    freq, expand = _rope_constants()
    for l in range(depth):
        x = _ffn_layer(x.reshape(bsz * seq, d), l, ffn_w[0][0], ffn_w[0][1], ln_w4, ln_b4, 0
                       ).reshape(bsz, seq, d)
        x = _mixer_layer(x, l, positions, freq, expand, win, bin_, attn_sinks[l], norm_w, conv_w, wb, wo,
                         ln_w4, ln_b4)
        x = _ffn_layer(x.reshape(bsz * seq, d), l, ffn_w[1][0], ffn_w[1][1], ln_w4, ln_b4, 2
                       ).reshape(bsz, seq, d)
    return x
```

```python
import functools
import math

import numpy as np
import jax
import jax.numpy as jnp
from jax import lax
from jax.experimental import pallas as pl
from jax.experimental.pallas import tpu as pltpu

D_MODEL = 1024
DEPTH = 2

HEAD_DIM = 64
N_Q_HEADS = 8
N_KV_HEADS = 2
GROUP = N_Q_HEADS // N_KV_HEADS
WINDOW = 128
ATTN_BLOCK = 128
ROPE_THETA = 500000.0
ROT_DIM = HEAD_DIM // 4
ATTN_WIDTH = N_Q_HEADS * HEAD_DIM
KV_WIDTH = N_KV_HEADS * HEAD_DIM

MLSTM_HEADS = 4
MLSTM_HEAD_DIM = D_MODEL // 8
MLSTM_WIDTH = MLSTM_HEADS * MLSTM_HEAD_DIM
MLSTM_CHUNK = 128

CONV_WIDTH = D_MODEL // 2
CONV_K = 3

N_BRANCH = 3
BRANCH_WIDTH = 512
D_FF = 2816
LN_EPS = 1e-5
ALPHA = (2.0 * DEPTH) ** 0.25
NEG_INF = -1e30

SIZES = (ATTN_WIDTH, KV_WIDTH, KV_WIDTH,
         MLSTM_WIDTH, MLSTM_WIDTH, MLSTM_WIDTH, MLSTM_WIDTH, MLSTM_HEADS, MLSTM_HEADS,
         CONV_WIDTH, CONV_WIDTH, CONV_WIDTH,
         N_BRANCH * D_MODEL)

LANES = 128
SUBLANES = 8
VMEM_LIMIT_BYTES = 56 * 1024 * 1024

OFF_AQ = 0
OFF_AK = OFF_AQ + ATTN_WIDTH
OFF_AV = OFF_AK + KV_WIDTH
OFF_M = OFF_AV + KV_WIDTH
OFF_C = OFF_M + 4 * MLSTM_WIDTH
OFF_G = OFF_C + 3 * CONV_WIDTH
OFF_IF = OFF_G + N_BRANCH * D_MODEL
P_PACKED = OFF_IF + 2 * LANES
GATE_PART = D_MODEL // 2

Q_HEAD_ORDER = (0, 4, 1, 5, 2, 6, 3, 7)

BF16 = jnp.bfloat16
F32 = jnp.float32


def _dot(a, b):
    return jnp.dot(a, b, preferred_element_type=F32)


def _dot_nt(a, b):
    return lax.dot_general(a, b, (((1,), (1,)), ((), ())), preferred_element_type=F32)


def _dot_tn(a, b):
    return lax.dot_general(a, b, (((0,), (0,)), ((), ())), preferred_element_type=F32)


def _layer_norm_rows(y, w, b):
    mu = jnp.mean(y, axis=-1, keepdims=True)
    yc = y - mu
    var = jnp.mean(yc * yc, axis=-1, keepdims=True)
    return yc * lax.rsqrt(var + LN_EPS) * w + b


def _sigmoid(x):
    return 0.5 * jnp.tanh(0.5 * x) + 0.5


def _log_sigmoid(x):
    return jnp.minimum(x, 0.0) - jnp.log1p(jnp.exp(-jnp.abs(x)))


def _ffn_kernel(x_ref, wgu_hbm, wd_hbm, lnw_ref, lnb_ref, o_ref,
                wgu_ref, wd_ref, stage_gu_ref, stage_d_ref, sem, *, chunks, layer):
    i = pl.program_id(0)
    d_ff = wd_ref.shape[0]

    def chunk_copies(k, slot):
        c0, c1 = chunks[k]
        return (
            pltpu.make_async_copy(wgu_hbm.at[layer, :, c0:c1], stage_gu_ref.at[slot, 0], sem.at[slot, 0]),
            pltpu.make_async_copy(wgu_hbm.at[layer, :, d_ff + c0:d_ff + c1], stage_gu_ref.at[slot, 1],
                                  sem.at[slot, 1]),
            pltpu.make_async_copy(wd_hbm.at[layer, c0:c1, :], stage_d_ref.at[slot], sem.at[slot, 2]),
        )

    def run(load_weights):
        x = x_ref[...]
        xb = x.astype(BF16)
        acc = None
        if load_weights:
            for cp in chunk_copies(0, 0):
                cp.start()
        for k, (c0, c1) in enumerate(chunks):
            if load_weights:
                slot = k % 2
                if k + 1 < len(chunks):
                    for cp in chunk_copies(k + 1, 1 - slot):
                        cp.start()
                for cp in chunk_copies(k, slot):
                    cp.wait()
                wgu_ref[:, c0:c1] = stage_gu_ref[slot, 0].astype(BF16)
                wgu_ref[:, d_ff + c0:d_ff + c1] = stage_gu_ref[slot, 1].astype(BF16)
                wd_ref[c0:c1, :] = stage_d_ref[slot].astype(BF16)
            g = _dot(xb, wgu_ref[:, c0:c1])
            u = _dot(xb, wgu_ref[:, d_ff + c0:d_ff + c1])
            a = (g * _sigmoid(g) * u).astype(BF16)
            part = _dot(a, wd_ref[c0:c1, :])
            acc = part if acc is None else acc + part
        y = ALPHA * x + 0.5 * acc
        o_ref[...] = _layer_norm_rows(y, lnw_ref[...], lnb_ref[...])

    @pl.when(i == 0)
    def _first():
        run(True)

    @pl.when(i > 0)
    def _rest():
        run(False)


def _ffn_tiles(n_tokens, d_ff):
    tm = 512 if n_tokens % 512 == 0 else 128
    step = 2 * LANES
    assert d_ff % step == 0
    chunks = tuple((c, c + step) for c in range(0, d_ff, step))
    return tm, chunks


def _ffn_layer(x2d, layer, w_gu, w_down, ln_w, ln_b, ln_idx):
    n, d = x2d.shape
    d_ff = w_down.shape[1]
    tm, chunks = _ffn_tiles(n, d_ff)
    step = chunks[0][1] - chunks[0][0]

    def const(shape, idx):
        return pl.BlockSpec(shape, lambda i: idx, pipeline_mode=pl.Buffered(1))

    return pl.pallas_call(
        functools.partial(_ffn_kernel, chunks=chunks, layer=layer),
        out_shape=jax.ShapeDtypeStruct((n, d), F32),
        grid=(n // tm,),
        in_specs=[
            pl.BlockSpec((tm, d), lambda i: (i, 0)),
            pl.BlockSpec(memory_space=pl.ANY),
            pl.BlockSpec(memory_space=pl.ANY),
            const((None, None, 1, d), (layer, ln_idx, 0, 0)),
            const((None, None, 1, d), (layer, ln_idx, 0, 0)),
        ],
        out_specs=pl.BlockSpec((tm, d), lambda i: (i, 0)),
        scratch_shapes=[
            pltpu.VMEM((d, 2 * d_ff), BF16),
            pltpu.VMEM((d_ff, d), BF16),
            pltpu.VMEM((2, 2, d, step), F32),
            pltpu.VMEM((2, step, d), F32),
            pltpu.SemaphoreType.DMA((2, 3)),
        ],
        compiler_params=pltpu.CompilerParams(
            dimension_semantics=("arbitrary",),
            vmem_limit_bytes=VMEM_LIMIT_BYTES),
        name="swiglu_ln",
    )(x2d, w_gu, w_down, ln_w, ln_b)


def _rope_tables(pos_row, freq, expand):
    ang = freq * pos_row.astype(F32)
    pieces = []
    for t in (jnp.cos(ang), jnp.sin(ang)):
        hi = t.astype(BF16)
        r1 = t - hi.astype(F32)
        mid = r1.astype(BF16)
        lo = (r1 - mid.astype(F32)).astype(BF16)
        pieces += [hi, mid, lo]
    pieces += [jnp.ones((SUBLANES, LANES), BF16), jnp.zeros((SUBLANES, LANES), BF16)]
    cs = _dot_tn(jnp.concatenate(pieces, axis=0), expand)
    return cs[:, 0:LANES], cs[:, LANES:]


def _rope(t, cos_f, sin_f):
    half = ROT_DIM // 2
    lane = lax.broadcasted_iota(jnp.int32, (1, LANES), 1)
    up = pltpu.roll(t, LANES - half, axis=1)
    down = pltpu.roll(t, half, axis=1)
    return t * cos_f + jnp.where((lane & (HEAD_DIM - 1)) < half, up, down) * sin_f


def _split_kv(prev, cur):
    lane = lax.broadcasted_iota(jnp.int32, (1, LANES), 1)
    half0 = lane < HEAD_DIM
    zero = jnp.zeros((), BF16)
    both = jnp.concatenate([prev, cur], axis=0)
    return jnp.concatenate([jnp.where(half0, both, zero),
                            jnp.where(half0, zero, both)], axis=0)


def _attention_scores(q_all, k_cur, k_prev):
    return _dot_nt(q_all, _split_kv(k_prev, k_cur))


def _attention_softmax(s_all, thr, sinks_ref):
    L = ATTN_BLOCK
    lane = lax.broadcasted_iota(jnp.int32, (1, LANES), 1)
    half0 = lane < HEAD_DIM
    qi = lax.broadcasted_iota(jnp.int32, (L, 2 * L), 0)
    kj = lax.broadcasted_iota(jnp.int32, (L, 2 * L), 1)
    rel = qi + L - kj
    mask = (rel >= 0) & (rel < WINDOW) & (kj >= thr)
    ps, scales = [], []
    for c in range(N_Q_HEADS // 2):
        pg, inv = [], []
        for g in range(N_KV_HEADS):
            sink = sinks_ref[Q_HEAD_ORDER[2 * c + g]]
            sg = jnp.where(mask, s_all[c * L:(c + 1) * L, g * 2 * L:(g + 1) * 2 * L], NEG_INF)
            m = jnp.maximum(jnp.max(sg, axis=-1, keepdims=True), sink)
            p = jnp.exp(sg - m)
            den = jnp.sum(p, axis=-1, keepdims=True) + jnp.exp(sink - m)
            pg.append(p.astype(BF16))
            inv.append(1.0 / den)
        ps.append(jnp.concatenate(pg, axis=1))
        scales.append(jnp.where(half0, inv[0], inv[1]))
    return jnp.concatenate(ps, axis=0), jnp.concatenate(scales, axis=0)


def _mlstm_gates(gif, row0):
    L = MLSTM_CHUNK
    rows = slice(row0, row0 + L)
    lane = lax.broadcasted_iota(jnp.int32, (1, LANES), 1)
    gi = gif[rows, 0:LANES]
    gf = gif[rows, LANES:2 * LANES]
    lf = jnp.where(lane < 2 * MLSTM_HEADS, _log_sigmoid(gf), 0.0)
    r_i = lax.broadcasted_iota(jnp.int32, (L, L), 0)
    c_i = lax.broadcasted_iota(jnp.int32, (L, L), 1)
    tril = jnp.where(r_i >= c_i, 1.0, 0.0).astype(BF16)
    hi = lf.astype(BF16)
    r1 = lf - hi.astype(F32)
    mid = r1.astype(BF16)
    lo = (r1 - mid.astype(F32)).astype(BF16)
    bsum = _dot(tril, jnp.concatenate([hi, mid, lo], axis=1))
    bsum = bsum[:, 0:LANES] + bsum[:, LANES:2 * LANES] + bsum[:, 2 * LANES:]
    t_col = jnp.where(lane < MLSTM_HEADS, gi - bsum, bsum)
    return t_col, t_col.T


def _mlstm_stage1(mqk, t_col, t_row, s_ref, m_ref, row0):
    L = MLSTM_CHUNK
    dh = MLSTM_HEAD_DIM
    rows = slice(row0, row0 + L)
    r_i = lax.broadcasted_iota(jnp.int32, (L, L), 0)
    c_i = lax.broadcasted_iota(jnp.int32, (L, L), 1)
    causal = r_i >= c_i
    scale = dh ** -0.5
    heads = []
    for h in range(MLSTM_HEADS):
        q = mqk[rows, h * dh:(h + 1) * dh].astype(BF16)
        k = mqk[rows, MLSTM_WIDTH + h * dh:MLSTM_WIDTH + (h + 1) * dh] * scale
        state = s_ref[h]
        qk = _dot_nt(q, k.astype(BF16))
        from_state = _dot(q, state.astype(BF16))
        heads.append(dict(k=k, state=state, qk=qk, from_state=from_state))
    for h, hd in enumerate(heads):
        b_col = t_col[:, MLSTM_HEADS + h:MLSTM_HEADS + h + 1]
        d_col = t_col[:, h:h + 1]
        d_row = t_row[h:h + 1, :]
        m_prev = m_ref[h][0:1, 0:1]
        logw = jnp.where(causal, b_col + d_row, NEG_INF)
        inter = b_col + m_prev
        m_t = jnp.maximum(inter, jnp.max(logw, axis=-1, keepdims=True))
        scores = hd["qk"] * jnp.exp(logw - m_t)
        m_new = m_t[L - 1:L, :]
        b_last = b_col[L - 1:L, :]
        ws = jnp.exp(b_last + d_col - m_new)
        hd.update(m_t=m_t, w_inter=jnp.exp(inter - m_t), scores=scores,
                  row_sum=jnp.sum(scores, axis=-1, keepdims=True),
                  decay=jnp.exp(b_last + m_prev - m_new), m_new=m_new,
                  kw=(hd["k"] * ws).astype(BF16))
    return heads


def _mlstm_stage2(mvo, heads, s_ref, m_ref, nw_ref, row0):
    L = MLSTM_CHUNK
    dh = MLSTM_HEAD_DIM
    rows = slice(row0, row0 + L)
    lane = lax.broadcasted_iota(jnp.int32, (1, LANES), 1)
    ones_tile = jnp.broadcast_to(jnp.where(lane == 0, 1.0, 0.0).astype(BF16), (L, LANES))
    outs = []
    for h, hd in enumerate(heads):
        v = mvo[rows, h * dh:(h + 1) * dh].astype(BF16)
        hd["intra"] = _dot(hd["scores"].astype(BF16), v)
        v_ext = jnp.concatenate([v, ones_tile], axis=1)
        s_ref[h] = hd["decay"] * hd["state"] + _dot_tn(hd["kw"], v_ext)
        m_ref[h] = jnp.broadcast_to(hd["m_new"], (SUBLANES, LANES))
    for h, hd in enumerate(heads):
        og = mvo[rows, MLSTM_WIDTH + h * dh:MLSTM_WIDTH + (h + 1) * dh]
        num = hd["w_inter"] * hd["from_state"][:, 0:dh] + hd["intra"]
        den = hd["w_inter"] * hd["from_state"][:, dh:dh + 1] + hd["row_sum"]
        hh = num / jnp.maximum(jnp.abs(den), jnp.exp(-hd["m_t"]))
        mu = jnp.mean(hh, axis=-1, keepdims=True)
        hc = hh - mu
        hn = hc * lax.rsqrt(jnp.mean(hc * hc, axis=-1, keepdims=True) + LN_EPS)
        outs.append(hn * nw_ref[:, h * dh:(h + 1) * dh] * _sigmoid(og))
    return outs


class _SubTile:
    def __init__(self, sub, tsub, s_idx, refs):
        self.sub, self.tsub, self.s_idx, self.r = sub, tsub, s_idx, refs
        self.r0 = sub * tsub
        self.nblk = tsub // ATTN_BLOCK
        self.x = refs["x"][0, self.r0:self.r0 + tsub, :]
        self.xb = self.x.astype(BF16)
        self.gates = [[None] * (D_MODEL // GATE_PART) for _ in range(N_BRANCH)]
        self.merged = None

    def proj(self, lo, hi):
        return _dot(self.xb, self.r["win"][:, lo:hi]) + self.r["bin"][:, lo:hi]

    def ropes(self):
        r, n = self.r, self.nblk
        self.rope = [_rope_tables(r["pos"][0, self.sub * n + j:self.sub * n + j + 1, :],
                                  r["freq"][...], r["expand"][...]) for j in range(n)]

    def proj_attn(self):
        self.qkv = self.proj(OFF_AQ, OFF_M)

    def proj_gates(self):
        gif = self.proj(OFF_IF, P_PACKED)
        self.mgates = [_mlstm_gates(gif, j * MLSTM_CHUNK) for j in range(self.nblk)]

    def proj_mlstm_qk(self):
        self.mqk = self.proj(OFF_M, OFF_M + 2 * MLSTM_WIDTH)

    def proj_mlstm_vo(self):
        self.mvo = self.proj(OFF_M + 2 * MLSTM_WIDTH, OFF_C)

    def conv(self):
        r, r0, tsub = self.r, self.r0, self.tsub
        cproj = self.proj(OFF_C, OFF_G)
        u = cproj[:, 2 * CONV_WIDTH:] * cproj[:, 0:CONV_WIDTH]
        r["conv"][SUBLANES + r0:SUBLANES + r0 + tsub, :] = u
        u1 = r["conv"][pl.ds(SUBLANES + r0 - 1, tsub), :]
        u2 = r["conv"][pl.ds(SUBLANES + r0 - 2, tsub), :]
        yc = r["cw"][0:1, :] * u2 + r["cw"][1:2, :] * u1 + r["cw"][2:3, :] * u
        r["y"][r0:r0 + tsub, 2 * BRANCH_WIDTH:] = (cproj[:, CONV_WIDTH:2 * CONV_WIDTH] * yc).astype(BF16)

    def gate(self, b, part):
        lo = OFF_G + b * D_MODEL + part * GATE_PART
        self.gates[b][part] = _sigmoid(self.proj(lo, lo + GATE_PART))

    def stage1(self, j, kv):
        L = ATTN_BLOCK
        rows = slice(j * L, (j + 1) * L)
        cos_f, sin_f = self.rope[j]
        q_all = jnp.concatenate(
            [(_rope(self.qkv[rows, c * LANES:(c + 1) * LANES], cos_f, sin_f)
              * (HEAD_DIM ** -0.5)).astype(BF16) for c in range(N_Q_HEADS // 2)], axis=0)
        self.k_cur = _rope(self.qkv[rows, OFF_AK:OFF_AK + LANES], cos_f, sin_f).astype(BF16)
        self.v_cur = self.qkv[rows, OFF_AV:OFF_AV + LANES].astype(BF16)
        thr = jnp.where(self.s_idx == 0, L, 0) if (self.sub == 0 and j == 0) else 0
        s_all = _attention_scores(q_all, self.k_cur, kv[0])
        self.heads = _mlstm_stage1(self.mqk, self.mgates[j][0], self.mgates[j][1],
                                   self.r["s"], self.r["m"], j * L)
        self.p_all, self.o_scale = _attention_softmax(s_all, thr, self.r["sinks"])

    def stage2(self, j, kv):
        L = ATTN_BLOCK
        r = self.r
        yrows = slice(self.r0 + j * L, self.r0 + (j + 1) * L)
        o_all = _dot(self.p_all, _split_kv(kv[1], self.v_cur)) * self.o_scale
        for c in range(N_Q_HEADS // 2):
            r["y"][yrows, c * LANES:(c + 1) * LANES] = o_all[c * L:(c + 1) * L, :].astype(BF16)
        outs = _mlstm_stage2(self.mvo, self.heads, r["s"], r["m"], r["nw"], j * L)
        for h, o in enumerate(outs):
            r["y"][yrows, BRANCH_WIDTH + h * MLSTM_HEAD_DIM:
                   BRANCH_WIDTH + (h + 1) * MLSTM_HEAD_DIM] = o.astype(BF16)
        return self.k_cur, self.v_cur

    def branch(self, b):
        r, r0, tsub = self.r, self.r0, self.tsub
        term = jnp.concatenate(self.gates[b], axis=1) * _dot(
            r["y"][r0:r0 + tsub, b * BRANCH_WIDTH:(b + 1) * BRANCH_WIDTH], r["wb"][b])
        self.merged = term if self.merged is None else self.merged + term

    def finish(self):
        r = self.r
        mix = _dot(self.merged.astype(BF16), r["wo"][...])
        y = ALPHA * self.x + mix
        r["o"][0, self.r0:self.r0 + self.tsub, :] = _layer_norm_rows(y, r["lnw"][...], r["lnb"][...])


def _mixer_kernel(sinks_ref, x_ref, pos_ref, freq_ref, expand_ref, win_ref, bin_ref, nw_ref, cw_ref,
                  wb_ref, wo_ref, lnw_ref, lnb_ref, o_ref,
                  kprev_ref, vprev_ref, s_ref, m_ref, conv_ref, y_ref, *, tm, tsub):
    s_idx = pl.program_id(1)

    @pl.when(s_idx == 0)
    def _reset():
        kprev_ref[...] = jnp.zeros_like(kprev_ref)
        vprev_ref[...] = jnp.zeros_like(vprev_ref)
        s_ref[...] = jnp.zeros_like(s_ref)
        m_ref[...] = jnp.zeros_like(m_ref)
        conv_ref[0:SUBLANES, :] = jnp.zeros((SUBLANES, CONV_WIDTH), F32)

    refs = dict(sinks=sinks_ref, x=x_ref, pos=pos_ref, freq=freq_ref, expand=expand_ref, win=win_ref,
                bin=bin_ref, nw=nw_ref, cw=cw_ref, wb=wb_ref, wo=wo_ref, lnw=lnw_ref, lnb=lnb_ref,
                o=o_ref, s=s_ref, m=m_ref, conv=conv_ref, y=y_ref)
    nsub = tm // tsub
    tiles = [_SubTile(t, tsub, s_idx, refs) for t in range(nsub)]

    def head_steps(t):
        return [t.ropes, t.proj_attn, t.proj_gates, t.proj_mlstm_qk, t.proj_mlstm_vo]

    def tail_steps(t):
        return [lambda: t.branch(2), lambda: t.branch(0), lambda: t.branch(1), t.finish]

    for step in head_steps(tiles[0]):
        step()
    kv = (kprev_ref[...], vprev_ref[...])
    carry_tail = []
    for i, t in enumerate(tiles):
        own = [t.conv] + [functools.partial(t.gate, b, p) for b in range(N_BRANCH)
                          for p in range(D_MODEL // GATE_PART)]
        nxt = head_steps(tiles[i + 1]) if i + 1 < nsub else []
        fillers = carry_tail + own + nxt
        slots = 2 * t.nblk
        for slot in range(slots):
            j = slot // 2
            if slot % 2 == 0:
                t.stage1(j, kv)
            else:
                kv = t.stage2(j, kv)
            take = -(-len(fillers) // (slots - slot))
            for step in fillers[:take]:
                step()
            fillers = fillers[take:]
        carry_tail = tail_steps(t)
    for step in carry_tail:
        step()
    kprev_ref[...], vprev_ref[...] = kv
    conv_ref[0:SUBLANES, :] = conv_ref[tm:tm + SUBLANES, :]


ROPE_K = 8 * SUBLANES


def _rope_constants():
    half = ROT_DIM // 2
    inv_freq = ROPE_THETA ** (-np.arange(0, ROT_DIM, 2, dtype=np.float64) / ROT_DIM)
    freq = np.repeat(inv_freq[:, None], LANES, axis=1).astype(np.float32)
    l64 = np.arange(LANES) % HEAD_DIM
    expand = np.zeros((ROPE_K, 2 * LANES), np.float32)
    for f in range(half):
        cos_lanes = np.where((l64 < ROT_DIM) & (l64 % half == f), 1.0, 0.0)
        sin_lanes = np.where(l64 == f, -1.0, 0.0) + np.where(l64 == f + half, 1.0, 0.0)
        for piece in range(3):
            expand[piece * SUBLANES + f, :LANES] = cos_lanes
            expand[(3 + piece) * SUBLANES + f, LANES:] = sin_lanes
    expand[6 * SUBLANES, :LANES] = np.where(l64 >= ROT_DIM, 1.0, 0.0)
    return jnp.asarray(freq), jnp.asarray(expand, dtype=BF16)


def _pack_in_proj(w):
    offs = np.concatenate([[0], np.cumsum(SIZES)]).tolist()
    seg = [w[..., offs[i]:offs[i + 1]] for i in range(len(SIZES))]
    (aq, ak, av, mq, mk, mv, mo, mi, mf, cx, cb, cc, gates) = seg
    aq_heads = [aq[..., h * HEAD_DIM:(h + 1) * HEAD_DIM] for h in Q_HEAD_ORDER]
    zi = jnp.zeros(w.shape[:-1] + (LANES - MLSTM_HEADS,), w.dtype)
    zf = jnp.zeros(w.shape[:-1] + (LANES - 2 * MLSTM_HEADS,), w.dtype)
    return jnp.concatenate(aq_heads + [ak, av, mq, mk, mv, mo, cx, cb, cc, gates,
                                       mi, zi, mf, mf, zf], axis=-1)


def _pack_kernel(wt_ref, o_ref):
    offs = np.concatenate([[0], np.cumsum(SIZES)]).tolist()
    lane = lax.broadcasted_iota(jnp.int32, (1, LANES), 1)

    def put(lo, src):
        o_ref[:, lo:lo + src.shape[0]] = src.T.astype(BF16)

    for c in range(N_Q_HEADS // 2):
        pair = [wt_ref[h * HEAD_DIM:(h + 1) * HEAD_DIM, :] for h in Q_HEAD_ORDER[2 * c:2 * c + 2]]
        put(c * LANES, jnp.concatenate(pair, axis=0))
    for lo in range(OFF_AK, OFF_C, 4 * LANES):
        hi = min(lo + 4 * LANES, OFF_C)
        put(lo, wt_ref[lo:hi, :])
    shift = offs[9] - OFF_C
    for lo in range(OFF_C, OFF_IF, 4 * LANES):
        hi = min(lo + 4 * LANES, OFF_IF)
        put(lo, wt_ref[lo + shift:hi + shift, :])
    t = wt_ref[offs[7]:offs[7] + LANES, :].T
    o_ref[:, OFF_IF:OFF_IF + LANES] = jnp.where(lane < MLSTM_HEADS, t, 0.0).astype(BF16)
    f_lo = pltpu.roll(t, LANES - MLSTM_HEADS, axis=1)
    f_tile = jnp.where(lane < MLSTM_HEADS, f_lo, jnp.where(lane < 2 * MLSTM_HEADS, t, 0.0))
    o_ref[:, OFF_IF + LANES:P_PACKED] = f_tile.astype(BF16)


def _pack_in_proj_weights(w_in):
    depth, d, p_in = w_in.shape
    rows = 256
    return pl.pallas_call(
        _pack_kernel,
        out_shape=jax.ShapeDtypeStruct((depth, d, P_PACKED), BF16),
        grid=(depth, d // rows),
        in_specs=[pl.BlockSpec((None, p_in, rows), lambda l, i: (l, 0, i))],
        out_specs=pl.BlockSpec((None, rows, P_PACKED), lambda l, i: (l, i, 0)),
        compiler_params=pltpu.CompilerParams(
            dimension_semantics=("arbitrary", "arbitrary"),
            vmem_limit_bytes=VMEM_LIMIT_BYTES),
        name="pack_in_proj",
    )(jnp.swapaxes(w_in, 1, 2))


def _pack_branch(w_branch):
    wb0 = jnp.concatenate([w_branch[:, 0:1, h * HEAD_DIM:(h + 1) * HEAD_DIM, :] for h in Q_HEAD_ORDER],
                          axis=2)
    return jnp.concatenate([wb0, w_branch[:, 1:]], axis=1)


def _mixer_tile(seq):
    if seq % 512 == 0:
        return 512, 256
    if seq % 256 == 0:
        return 256, 256
    return ATTN_BLOCK, ATTN_BLOCK


def _mixer_layer(x, layer, positions, freq, expand, win, bin_, sinks, norm_w, conv_w, wb, wo, ln_w, ln_b):
    bsz, seq, d = x.shape
    tm, tsub = _mixer_tile(seq)
    ns = seq // tm
    nblk = tm // ATTN_BLOCK
    pos = positions.reshape(bsz * ns, nblk, ATTN_BLOCK)

    def const(shape, idx):
        return pl.BlockSpec(shape, lambda b, s, *_: idx, pipeline_mode=pl.Buffered(1))

    grid_spec = pltpu.PrefetchScalarGridSpec(
        num_scalar_prefetch=1,
        grid=(bsz, ns),
        in_specs=[
            pl.BlockSpec((1, tm, d), lambda b, s, *_: (b, s, 0)),
            pl.BlockSpec((1, nblk, ATTN_BLOCK), lambda b, s, *_: (b * ns + s, 0, 0)),
            const((SUBLANES, LANES), (0, 0)),
            const((ROPE_K, 2 * LANES), (0, 0)),
            const((None, d, P_PACKED), (layer, 0, 0)),
            const((None, 1, P_PACKED), (layer, 0, 0)),
            const((None, 1, MLSTM_WIDTH), (layer, 0, 0)),
            const((None, CONV_K, CONV_WIDTH), (layer, 0, 0)),
            const((None, N_BRANCH, BRANCH_WIDTH, d), (layer, 0, 0, 0)),
            const((None, d, d), (layer, 0, 0)),
            const((None, None, 1, d), (layer, 1, 0, 0)),
            const((None, None, 1, d), (layer, 1, 0, 0)),
        ],
        out_specs=pl.BlockSpec((1, tm, d), lambda b, s, *_: (b, s, 0)),
        scratch_shapes=[
            pltpu.VMEM((ATTN_BLOCK, LANES), BF16),
            pltpu.VMEM((ATTN_BLOCK, LANES), BF16),
            pltpu.VMEM((MLSTM_HEADS, MLSTM_HEAD_DIM, 2 * MLSTM_HEAD_DIM), F32),
            pltpu.VMEM((MLSTM_HEADS, SUBLANES, LANES), F32),
            pltpu.VMEM((tm + 2 * SUBLANES, CONV_WIDTH), F32),
            pltpu.VMEM((tm, N_BRANCH * BRANCH_WIDTH), BF16),
        ],
    )
    return pl.pallas_call(
        functools.partial(_mixer_kernel, tm=tm, tsub=tsub),
        out_shape=jax.ShapeDtypeStruct((bsz, seq, d), F32),
        grid_spec=grid_spec,
        compiler_params=pltpu.CompilerParams(
            dimension_semantics=("arbitrary", "arbitrary"),
            vmem_limit_bytes=VMEM_LIMIT_BYTES),
        name="token_mixer",
    )(sinks, x, pos, freq, expand, win, bin_, norm_w, conv_w, wb, wo, ln_w, ln_b)


def kernel(x, positions, w_in, b_in, attn_sinks, mlstm_norm_w, conv_w, w_branch, w_out,
           ffn1_w_gu, ffn1_w_down, ffn2_w_gu, ffn2_w_down, ln_w, ln_b):
    bsz, seq, d = x.shape
    depth = w_in.shape[0]
    ffn_w = ((ffn1_w_gu, ffn1_w_down), (ffn2_w_gu, ffn2_w_down))
    win = _pack_in_proj_weights(w_in)
    bin_ = _pack_in_proj(b_in)[:, None, :]
    wb = _pack_branch(w_branch.astype(BF16))
    wo = w_out.astype(BF16)
    norm_w = mlstm_norm_w[:, None, :]
    ln_w4 = ln_w[:, :, None, :]
    ln_b4 = ln_b[:, :, None, :]
    freq, expand = _rope_constants()
    for l in range(depth):
        x = _ffn_layer(x.reshape(bsz * seq, d), l, ffn_w[0][0], ffn_w[0][1], ln_w4, ln_b4, 0
                       ).reshape(bsz, seq, d)
        x = _mixer_layer(x, l, positions, freq, expand, win, bin_, attn_sinks[l], norm_w, conv_w, wb, wo,
                         ln_w4, ln_b4)
        x = _ffn_layer(x.reshape(bsz * seq, d), l, ffn_w[1][0], ffn_w[1][1], ln_w4, ln_b4, 2
                       ).reshape(bsz, seq, d)
    return x
```

```python
import functools
import math

import numpy as np
import jax
import jax.numpy as jnp
from jax import lax
from jax.experimental import pallas as pl
from jax.experimental.pallas import tpu as pltpu

D_MODEL = 1024
DEPTH = 2

HEAD_DIM = 64
N_Q_HEADS = 8
N_KV_HEADS = 2
GROUP = N_Q_HEADS // N_KV_HEADS
WINDOW = 128
ATTN_BLOCK = 128
ROPE_THETA = 500000.0
ROT_DIM = HEAD_DIM // 4
ATTN_WIDTH = N_Q_HEADS * HEAD_DIM
KV_WIDTH = N_KV_HEADS * HEAD_DIM

MLSTM_HEADS = 4
MLSTM_HEAD_DIM = D_MODEL // 8
MLSTM_WIDTH = MLSTM_HEADS * MLSTM_HEAD_DIM
MLSTM_CHUNK = 128

CONV_WIDTH = D_MODEL // 2
CONV_K = 3

N_BRANCH = 3
BRANCH_WIDTH = 512
D_FF = 2816
LN_EPS = 1e-5
ALPHA = (2.0 * DEPTH) ** 0.25
NEG_INF = -1e30

SIZES = (ATTN_WIDTH, KV_WIDTH, KV_WIDTH,
         MLSTM_WIDTH, MLSTM_WIDTH, MLSTM_WIDTH, MLSTM_WIDTH, MLSTM_HEADS, MLSTM_HEADS,
         CONV_WIDTH, CONV_WIDTH, CONV_WIDTH,
         N_BRANCH * D_MODEL)

LANES = 128
SUBLANES = 8
VMEM_LIMIT_BYTES = 56 * 1024 * 1024

OFF_AQ = 0
OFF_AK = OFF_AQ + ATTN_WIDTH
OFF_AV = OFF_AK + KV_WIDTH
OFF_M = OFF_AV + KV_WIDTH
OFF_C = OFF_M + 4 * MLSTM_WIDTH
OFF_G = OFF_C + 3 * CONV_WIDTH
OFF_IF = OFF_G + N_BRANCH * D_MODEL
P_PACKED = OFF_IF + 2 * LANES

Q_HEAD_ORDER = (0, 4, 1, 5, 2, 6, 3, 7)

BF16 = jnp.bfloat16
F32 = jnp.float32


def _dot(a, b):
    return jnp.dot(a, b, preferred_element_type=F32)


def _dot_nt(a, b):
    return lax.dot_general(a, b, (((1,), (1,)), ((), ())), preferred_element_type=F32)


def _dot_tn(a, b):
    return lax.dot_general(a, b, (((0,), (0,)), ((), ())), preferred_element_type=F32)


def _layer_norm_rows(y, w, b):
    mu = jnp.mean(y, axis=-1, keepdims=True)
    yc = y - mu
    var = jnp.mean(yc * yc, axis=-1, keepdims=True)
    return yc * lax.rsqrt(var + LN_EPS) * w + b


def _sigmoid(x):
    return 0.5 * jnp.tanh(0.5 * x) + 0.5


def _log_sigmoid(x):
    return jnp.minimum(x, 0.0) - jnp.log1p(jnp.exp(-jnp.abs(x)))


def _ffn_kernel(x_ref, wgu_hbm, wd_hbm, lnw_ref, lnb_ref, o_ref,
                wgu_ref, wd_ref, stage_gu_ref, stage_d_ref, sem, *, chunks, layer):
    i = pl.program_id(0)
    d_ff = wd_ref.shape[0]

    def chunk_copies(k, slot):
        c0, c1 = chunks[k]
        return (
            pltpu.make_async_copy(wgu_hbm.at[layer, :, c0:c1], stage_gu_ref.at[slot, 0], sem.at[slot, 0]),
            pltpu.make_async_copy(wgu_hbm.at[layer, :, d_ff + c0:d_ff + c1], stage_gu_ref.at[slot, 1],
                                  sem.at[slot, 1]),
            pltpu.make_async_copy(wd_hbm.at[layer, c0:c1, :], stage_d_ref.at[slot], sem.at[slot, 2]),
        )

    def run(load_weights):
        x = x_ref[...]
        xb = x.astype(BF16)
        acc = None
        if load_weights:
            for cp in chunk_copies(0, 0):
                cp.start()
        for k, (c0, c1) in enumerate(chunks):
            if load_weights:
                slot = k % 2
                if k + 1 < len(chunks):
                    for cp in chunk_copies(k + 1, 1 - slot):
                        cp.start()
                for cp in chunk_copies(k, slot):
                    cp.wait()
                wgu_ref[:, c0:c1] = stage_gu_ref[slot, 0].astype(BF16)
                wgu_ref[:, d_ff + c0:d_ff + c1] = stage_gu_ref[slot, 1].astype(BF16)
                wd_ref[c0:c1, :] = stage_d_ref[slot].astype(BF16)
            g = _dot(xb, wgu_ref[:, c0:c1])
            u = _dot(xb, wgu_ref[:, d_ff + c0:d_ff + c1])
            a = (g * _sigmoid(g) * u).astype(BF16)
            part = _dot(a, wd_ref[c0:c1, :])
            acc = part if acc is None else acc + part
        y = ALPHA * x + 0.5 * acc
        o_ref[...] = _layer_norm_rows(y, lnw_ref[...], lnb_ref[...])

    @pl.when(i == 0)
    def _first():
        run(True)

    @pl.when(i > 0)
    def _rest():
        run(False)


def _ffn_tiles(n_tokens, d_ff):
    tm = 512 if n_tokens % 512 == 0 else 128
    step = 2 * LANES
    assert d_ff % step == 0
    chunks = tuple((c, c + step) for c in range(0, d_ff, step))
    return tm, chunks


def _ffn_layer(x2d, layer, w_gu, w_down, ln_w, ln_b, ln_idx):
    n, d = x2d.shape
    d_ff = w_down.shape[1]
    tm, chunks = _ffn_tiles(n, d_ff)
    step = chunks[0][1] - chunks[0][0]

    def const(shape, idx):
        return pl.BlockSpec(shape, lambda i: idx, pipeline_mode=pl.Buffered(1))

    return pl.pallas_call(
        functools.partial(_ffn_kernel, chunks=chunks, layer=layer),
        out_shape=jax.ShapeDtypeStruct((n, d), F32),
        grid=(n // tm,),
        in_specs=[
            pl.BlockSpec((tm, d), lambda i: (i, 0)),
            pl.BlockSpec(memory_space=pl.ANY),
            pl.BlockSpec(memory_space=pl.ANY),
            const((None, None, 1, d), (layer, ln_idx, 0, 0)),
            const((None, None, 1, d), (layer, ln_idx, 0, 0)),
        ],
        out_specs=pl.BlockSpec((tm, d), lambda i: (i, 0)),
        scratch_shapes=[
            pltpu.VMEM((d, 2 * d_ff), BF16),
            pltpu.VMEM((d_ff, d), BF16),
            pltpu.VMEM((2, 2, d, step), F32),
            pltpu.VMEM((2, step, d), F32),
            pltpu.SemaphoreType.DMA((2, 3)),
        ],
        compiler_params=pltpu.CompilerParams(
            dimension_semantics=("arbitrary",),
            vmem_limit_bytes=VMEM_LIMIT_BYTES),
        name="swiglu_ln",
    )(x2d, w_gu, w_down, ln_w, ln_b)


def _rope_tables(pos_row, freq, expand):
    ang = freq * pos_row.astype(F32)
    pieces = []
    for t in (jnp.cos(ang), jnp.sin(ang)):
        hi = t.astype(BF16)
        r1 = t - hi.astype(F32)
        mid = r1.astype(BF16)
        lo = (r1 - mid.astype(F32)).astype(BF16)
        pieces += [hi, mid, lo]
    pieces += [jnp.ones((SUBLANES, LANES), BF16), jnp.zeros((SUBLANES, LANES), BF16)]
    cs = _dot_tn(jnp.concatenate(pieces, axis=0), expand)
    return cs[:, 0:LANES], cs[:, LANES:]


def _rope(t, cos_f, sin_f):
    half = ROT_DIM // 2
    lane = lax.broadcasted_iota(jnp.int32, (1, LANES), 1)
    up = pltpu.roll(t, LANES - half, axis=1)
    down = pltpu.roll(t, half, axis=1)
    return t * cos_f + jnp.where((lane & (HEAD_DIM - 1)) < half, up, down) * sin_f


def _split_kv(prev, cur):
    lane = lax.broadcasted_iota(jnp.int32, (1, LANES), 1)
    half0 = lane < HEAD_DIM
    zero = jnp.zeros((), BF16)
    both = jnp.concatenate([prev, cur], axis=0)
    return jnp.concatenate([jnp.where(half0, both, zero),
                            jnp.where(half0, zero, both)], axis=0)


def _attention_scores(q_all, k_cur, k_prev):
    return _dot_nt(q_all, _split_kv(k_prev, k_cur))


def _attention_softmax(s_all, thr, sinks_ref):
    L = ATTN_BLOCK
    lane = lax.broadcasted_iota(jnp.int32, (1, LANES), 1)
    half0 = lane < HEAD_DIM
    qi = lax.broadcasted_iota(jnp.int32, (L, 2 * L), 0)
    kj = lax.broadcasted_iota(jnp.int32, (L, 2 * L), 1)
    rel = qi + L - kj
    mask = (rel >= 0) & (rel < WINDOW) & (kj >= thr)
    ps, scales = [], []
    for c in range(N_Q_HEADS // 2):
        pg, inv = [], []
        for g in range(N_KV_HEADS):
            sink = sinks_ref[Q_HEAD_ORDER[2 * c + g]]
            sg = jnp.where(mask, s_all[c * L:(c + 1) * L, g * 2 * L:(g + 1) * 2 * L], NEG_INF)
            m = jnp.maximum(jnp.max(sg, axis=-1, keepdims=True), sink)
            p = jnp.exp(sg - m)
            den = jnp.sum(p, axis=-1, keepdims=True) + jnp.exp(sink - m)
            pg.append(p.astype(BF16))
            inv.append(1.0 / den)
        ps.append(jnp.concatenate(pg, axis=1))
        scales.append(jnp.where(half0, inv[0], inv[1]))
    return jnp.concatenate(ps, axis=0), jnp.concatenate(scales, axis=0)


def _mlstm_gates(gif, row0):
    L = MLSTM_CHUNK
    rows = slice(row0, row0 + L)
    lane = lax.broadcasted_iota(jnp.int32, (1, LANES), 1)
    gi = gif[rows, 0:LANES]
    gf = gif[rows, LANES:2 * LANES]
    lf = jnp.where(lane < 2 * MLSTM_HEADS, _log_sigmoid(gf), 0.0)
    r_i = lax.broadcasted_iota(jnp.int32, (L, L), 0)
    c_i = lax.broadcasted_iota(jnp.int32, (L, L), 1)
    tril = jnp.where(r_i >= c_i, 1.0, 0.0).astype(BF16)
    hi = lf.astype(BF16)
    r1 = lf - hi.astype(F32)
    mid = r1.astype(BF16)
    lo = (r1 - mid.astype(F32)).astype(BF16)
    bsum = _dot(tril, jnp.concatenate([hi, mid, lo], axis=1))
    bsum = bsum[:, 0:LANES] + bsum[:, LANES:2 * LANES] + bsum[:, 2 * LANES:]
    t_col = jnp.where(lane < MLSTM_HEADS, gi - bsum, bsum)
    return t_col, t_col.T


def _mlstm_stage1(mqk, t_col, t_row, s_ref, m_ref, row0):
    L = MLSTM_CHUNK
    dh = MLSTM_HEAD_DIM
    rows = slice(row0, row0 + L)
    r_i = lax.broadcasted_iota(jnp.int32, (L, L), 0)
    c_i = lax.broadcasted_iota(jnp.int32, (L, L), 1)
    causal = r_i >= c_i
    scale = dh ** -0.5
    heads = []
    for h in range(MLSTM_HEADS):
        q = mqk[rows, h * dh:(h + 1) * dh].astype(BF16)
        k = mqk[rows, MLSTM_WIDTH + h * dh:MLSTM_WIDTH + (h + 1) * dh] * scale
        state = s_ref[h]
        qk = _dot_nt(q, k.astype(BF16))
        from_state = _dot(q, state.astype(BF16))
        heads.append(dict(k=k, state=state, qk=qk, from_state=from_state))
    for h, hd in enumerate(heads):
        b_col = t_col[:, MLSTM_HEADS + h:MLSTM_HEADS + h + 1]
        d_col = t_col[:, h:h + 1]
        d_row = t_row[h:h + 1, :]
        m_prev = m_ref[h][0:1, 0:1]
        logw = jnp.where(causal, b_col + d_row, NEG_INF)
        inter = b_col + m_prev
        m_t = jnp.maximum(inter, jnp.max(logw, axis=-1, keepdims=True))
        scores = hd["qk"] * jnp.exp(logw - m_t)
        m_new = m_t[L - 1:L, :]
        b_last = b_col[L - 1:L, :]
        ws = jnp.exp(b_last + d_col - m_new)
        hd.update(m_t=m_t, w_inter=jnp.exp(inter - m_t), scores=scores,
                  row_sum=jnp.sum(scores, axis=-1, keepdims=True),
                  decay=jnp.exp(b_last + m_prev - m_new), m_new=m_new,
                  kw=(hd["k"] * ws).astype(BF16))
    return heads


def _mlstm_stage2(mvo, heads, s_ref, m_ref, nw_ref, row0):
    L = MLSTM_CHUNK
    dh = MLSTM_HEAD_DIM
    rows = slice(row0, row0 + L)
    lane = lax.broadcasted_iota(jnp.int32, (1, LANES), 1)
    ones_tile = jnp.broadcast_to(jnp.where(lane == 0, 1.0, 0.0).astype(BF16), (L, LANES))
    outs = []
    for h, hd in enumerate(heads):
        v = mvo[rows, h * dh:(h + 1) * dh].astype(BF16)
        hd["intra"] = _dot(hd["scores"].astype(BF16), v)
        v_ext = jnp.concatenate([v, ones_tile], axis=1)
        s_ref[h] = hd["decay"] * hd["state"] + _dot_tn(hd["kw"], v_ext)
        m_ref[h] = jnp.broadcast_to(hd["m_new"], (SUBLANES, LANES))
    for h, hd in enumerate(heads):
        og = mvo[rows, MLSTM_WIDTH + h * dh:MLSTM_WIDTH + (h + 1) * dh]
        num = hd["w_inter"] * hd["from_state"][:, 0:dh] + hd["intra"]
        den = hd["w_inter"] * hd["from_state"][:, dh:dh + 1] + hd["row_sum"]
        hh = num / jnp.maximum(jnp.abs(den), jnp.exp(-hd["m_t"]))
        mu = jnp.mean(hh, axis=-1, keepdims=True)
        hc = hh - mu
        hn = hc * lax.rsqrt(jnp.mean(hc * hc, axis=-1, keepdims=True) + LN_EPS)
        outs.append(hn * nw_ref[:, h * dh:(h + 1) * dh] * _sigmoid(og))
    return outs


class _SubTile:
    def __init__(self, sub, tsub, s_idx, refs):
        self.sub, self.tsub, self.s_idx, self.r = sub, tsub, s_idx, refs
        self.r0 = sub * tsub
        self.nblk = tsub // ATTN_BLOCK
        self.x = refs["x"][0, self.r0:self.r0 + tsub, :]
        self.xb = self.x.astype(BF16)
        self.gates = None
        self.merged = None

    def proj(self, lo, hi):
        return _dot(self.xb, self.r["win"][:, lo:hi]) + self.r["bin"][:, lo:hi]

    def ropes(self):
        r, n = self.r, self.nblk
        self.rope = [_rope_tables(r["pos"][0, self.sub * n + j:self.sub * n + j + 1, :],
                                  r["freq"][...], r["expand"][...]) for j in range(n)]

    def proj_head(self):
        p = self.proj(OFF_AQ, OFF_C)
        self.qkv = p[:, OFF_AQ:OFF_M]
        self.mqk = p[:, OFF_M:OFF_M + 2 * MLSTM_WIDTH]
        self.mvo = p[:, OFF_M + 2 * MLSTM_WIDTH:OFF_C]

    def proj_gates(self):
        gif = self.proj(OFF_IF, P_PACKED)
        self.mgates = [_mlstm_gates(gif, j * MLSTM_CHUNK) for j in range(self.nblk)]

    def conv(self):
        r, r0, tsub = self.r, self.r0, self.tsub
        cproj = self.proj(OFF_C, OFF_G)
        u = cproj[:, 2 * CONV_WIDTH:] * cproj[:, 0:CONV_WIDTH]
        r["conv"][SUBLANES + r0:SUBLANES + r0 + tsub, :] = u
        u1 = r["conv"][pl.ds(SUBLANES + r0 - 1, tsub), :]
        u2 = r["conv"][pl.ds(SUBLANES + r0 - 2, tsub), :]
        yc = r["cw"][0:1, :] * u2 + r["cw"][1:2, :] * u1 + r["cw"][2:3, :] * u
        r["y"][r0:r0 + tsub, 2 * BRANCH_WIDTH:] = (cproj[:, CONV_WIDTH:2 * CONV_WIDTH] * yc).astype(BF16)

    def branch_gates(self):
        g = _sigmoid(self.proj(OFF_G, OFF_IF))
        self.gates = [g[:, b * D_MODEL:(b + 1) * D_MODEL] for b in range(N_BRANCH)]

    def stage1(self, j, kv):
        L = ATTN_BLOCK
        rows = slice(j * L, (j + 1) * L)
        cos_f, sin_f = self.rope[j]
        q_all = jnp.concatenate(
            [(_rope(self.qkv[rows, c * LANES:(c + 1) * LANES], cos_f, sin_f)
              * (HEAD_DIM ** -0.5)).astype(BF16) for c in range(N_Q_HEADS // 2)], axis=0)
        self.k_cur = _rope(self.qkv[rows, OFF_AK:OFF_AK + LANES], cos_f, sin_f).astype(BF16)
        self.v_cur = self.qkv[rows, OFF_AV:OFF_AV + LANES].astype(BF16)
        thr = jnp.where(self.s_idx == 0, L, 0) if (self.sub == 0 and j == 0) else 0
        s_all = _attention_scores(q_all, self.k_cur, kv[0])
        self.heads = _mlstm_stage1(self.mqk, self.mgates[j][0], self.mgates[j][1],
                                   self.r["s"], self.r["m"], j * L)
        self.p_all, self.o_scale = _attention_softmax(s_all, thr, self.r["sinks"])

    def stage2(self, j, kv):
        L = ATTN_BLOCK
        r = self.r
        yrows = slice(self.r0 + j * L, self.r0 + (j + 1) * L)
        o_all = _dot(self.p_all, _split_kv(kv[1], self.v_cur)) * self.o_scale
        for c in range(N_Q_HEADS // 2):
            r["y"][yrows, c * LANES:(c + 1) * LANES] = o_all[c * L:(c + 1) * L, :].astype(BF16)
        outs = _mlstm_stage2(self.mvo, self.heads, r["s"], r["m"], r["nw"], j * L)
        for h, o in enumerate(outs):
            r["y"][yrows, BRANCH_WIDTH + h * MLSTM_HEAD_DIM:
                   BRANCH_WIDTH + (h + 1) * MLSTM_HEAD_DIM] = o.astype(BF16)
        return self.k_cur, self.v_cur

    def branch(self, b):
        r, r0, tsub = self.r, self.r0, self.tsub
        term = self.gates[b] * _dot(
            r["y"][r0:r0 + tsub, b * BRANCH_WIDTH:(b + 1) * BRANCH_WIDTH], r["wb"][b])
        self.merged = term if self.merged is None else self.merged + term

    def finish(self):
        r = self.r
        mix = _dot(self.merged.astype(BF16), r["wo"][...])
        y = ALPHA * self.x + mix
        r["o"][0, self.r0:self.r0 + self.tsub, :] = _layer_norm_rows(y, r["lnw"][...], r["lnb"][...])


def _mixer_kernel(sinks_ref, x_ref, pos_ref, freq_ref, expand_ref, win_ref, bin_ref, nw_ref, cw_ref,
                  wb_ref, wo_ref, lnw_ref, lnb_ref, o_ref,
                  kprev_ref, vprev_ref, s_ref, m_ref, conv_ref, y_ref, *, tm, tsub):
    s_idx = pl.program_id(1)

    @pl.when(s_idx == 0)
    def _reset():
        kprev_ref[...] = jnp.zeros_like(kprev_ref)
        vprev_ref[...] = jnp.zeros_like(vprev_ref)
        s_ref[...] = jnp.zeros_like(s_ref)
        m_ref[...] = jnp.zeros_like(m_ref)
        conv_ref[0:SUBLANES, :] = jnp.zeros((SUBLANES, CONV_WIDTH), F32)

    refs = dict(sinks=sinks_ref, x=x_ref, pos=pos_ref, freq=freq_ref, expand=expand_ref, win=win_ref,
                bin=bin_ref, nw=nw_ref, cw=cw_ref, wb=wb_ref, wo=wo_ref, lnw=lnw_ref, lnb=lnb_ref,
                o=o_ref, s=s_ref, m=m_ref, conv=conv_ref, y=y_ref)
    nsub = tm // tsub
    tiles = [_SubTile(t, tsub, s_idx, refs) for t in range(nsub)]

    def head_steps(t):
        return [t.ropes, t.proj_head, t.proj_gates]

    def tail_steps(t):
        return [lambda: t.branch(2), lambda: t.branch(0), lambda: t.branch(1), t.finish]

    for step in head_steps(tiles[0]):
        step()
    kv = (kprev_ref[...], vprev_ref[...])
    carry_tail = []
    for i, t in enumerate(tiles):
        own = [t.conv, t.branch_gates]
        nxt = head_steps(tiles[i + 1]) if i + 1 < nsub else []
        fillers = carry_tail + own + nxt
        slots = 2 * t.nblk
        for slot in range(slots):
            j = slot // 2
            if slot % 2 == 0:
                t.stage1(j, kv)
            else:
                kv = t.stage2(j, kv)
            take = -(-len(fillers) // (slots - slot))
            for step in fillers[:take]:
                step()
            fillers = fillers[take:]
        carry_tail = tail_steps(t)
    for step in carry_tail:
        step()
    kprev_ref[...], vprev_ref[...] = kv
    conv_ref[0:SUBLANES, :] = conv_ref[tm:tm + SUBLANES, :]


ROPE_K = 8 * SUBLANES


def _rope_constants():
    half = ROT_DIM // 2
    inv_freq = ROPE_THETA ** (-np.arange(0, ROT_DIM, 2, dtype=np.float64) / ROT_DIM)
    freq = np.repeat(inv_freq[:, None], LANES, axis=1).astype(np.float32)
    l64 = np.arange(LANES) % HEAD_DIM
    expand = np.zeros((ROPE_K, 2 * LANES), np.float32)
    for f in range(half):
        cos_lanes = np.where((l64 < ROT_DIM) & (l64 % half == f), 1.0, 0.0)
        sin_lanes = np.where(l64 == f, -1.0, 0.0) + np.where(l64 == f + half, 1.0, 0.0)
        for piece in range(3):
            expand[piece * SUBLANES + f, :LANES] = cos_lanes
            expand[(3 + piece) * SUBLANES + f, LANES:] = sin_lanes
    expand[6 * SUBLANES, :LANES] = np.where(l64 >= ROT_DIM, 1.0, 0.0)
    return jnp.asarray(freq), jnp.asarray(expand, dtype=BF16)


def _pack_in_proj(w):
    offs = np.concatenate([[0], np.cumsum(SIZES)]).tolist()
    seg = [w[..., offs[i]:offs[i + 1]] for i in range(len(SIZES))]
    (aq, ak, av, mq, mk, mv, mo, mi, mf, cx, cb, cc, gates) = seg
    aq_heads = [aq[..., h * HEAD_DIM:(h + 1) * HEAD_DIM] for h in Q_HEAD_ORDER]
    zi = jnp.zeros(w.shape[:-1] + (LANES - MLSTM_HEADS,), w.dtype)
    zf = jnp.zeros(w.shape[:-1] + (LANES - 2 * MLSTM_HEADS,), w.dtype)
    return jnp.concatenate(aq_heads + [ak, av, mq, mk, mv, mo, cx, cb, cc, gates,
                                       mi, zi, mf, mf, zf], axis=-1)


def _pack_kernel(wt_ref, o_ref):
    offs = np.concatenate([[0], np.cumsum(SIZES)]).tolist()
    lane = lax.broadcasted_iota(jnp.int32, (1, LANES), 1)

    def put(lo, src):
        o_ref[:, lo:lo + src.shape[0]] = src.T.astype(BF16)

    for c in range(N_Q_HEADS // 2):
        pair = [wt_ref[h * HEAD_DIM:(h + 1) * HEAD_DIM, :] for h in Q_HEAD_ORDER[2 * c:2 * c + 2]]
        put(c * LANES, jnp.concatenate(pair, axis=0))
    for lo in range(OFF_AK, OFF_C, 4 * LANES):
        hi = min(lo + 4 * LANES, OFF_C)
        put(lo, wt_ref[lo:hi, :])
    shift = offs[9] - OFF_C
    for lo in range(OFF_C, OFF_IF, 4 * LANES):
        hi = min(lo + 4 * LANES, OFF_IF)
        put(lo, wt_ref[lo + shift:hi + shift, :])
    t = wt_ref[offs[7]:offs[7] + LANES, :].T
    o_ref[:, OFF_IF:OFF_IF + LANES] = jnp.where(lane < MLSTM_HEADS, t, 0.0).astype(BF16)
    f_lo = pltpu.roll(t, LANES - MLSTM_HEADS, axis=1)
    f_tile = jnp.where(lane < MLSTM_HEADS, f_lo, jnp.where(lane < 2 * MLSTM_HEADS, t, 0.0))
    o_ref[:, OFF_IF + LANES:P_PACKED] = f_tile.astype(BF16)


def _pack_in_proj_weights(w_in):
    depth, d, p_in = w_in.shape
    rows = 256
    return pl.pallas_call(
        _pack_kernel,
        out_shape=jax.ShapeDtypeStruct((depth, d, P_PACKED), BF16),
        grid=(depth, d // rows),
        in_specs=[pl.BlockSpec((None, p_in, rows), lambda l, i: (l, 0, i))],
        out_specs=pl.BlockSpec((None, rows, P_PACKED), lambda l, i: (l, i, 0)),
        compiler_params=pltpu.CompilerParams(
            dimension_semantics=("arbitrary", "arbitrary"),
            vmem_limit_bytes=VMEM_LIMIT_BYTES),
        name="pack_in_proj",
    )(jnp.swapaxes(w_in, 1, 2))


def _pack_branch(w_branch):
    wb0 = jnp.concatenate([w_branch[:, 0:1, h * HEAD_DIM:(h + 1) * HEAD_DIM, :] for h in Q_HEAD_ORDER],
                          axis=2)
    return jnp.concatenate([wb0, w_branch[:, 1:]], axis=1)


def _mixer_tile(seq):
    if seq % 512 == 0:
        return 512, 256
    if seq % 256 == 0:
        return 256, 256
    return ATTN_BLOCK, ATTN_BLOCK


def _mixer_layer(x, layer, positions, freq, expand, win, bin_, sinks, norm_w, conv_w, wb, wo, ln_w, ln_b):
    bsz, seq, d = x.shape
    tm, tsub = _mixer_tile(seq)
    ns = seq // tm
    nblk = tm // ATTN_BLOCK
    pos = positions.reshape(bsz * ns, nblk, ATTN_BLOCK)

    def const(shape, idx):
        return pl.BlockSpec(shape, lambda b, s, *_: idx, pipeline_mode=pl.Buffered(1))

    grid_spec = pltpu.PrefetchScalarGridSpec(
        num_scalar_prefetch=1,
        grid=(bsz, ns),
        in_specs=[
            pl.BlockSpec((1, tm, d), lambda b, s, *_: (b, s, 0)),
            pl.BlockSpec((1, nblk, ATTN_BLOCK), lambda b, s, *_: (b * ns + s, 0, 0)),
            const((SUBLANES, LANES), (0, 0)),
            const((ROPE_K, 2 * LANES), (0, 0)),
            const((None, d, P_PACKED), (layer, 0, 0)),
            const((None, 1, P_PACKED), (layer, 0, 0)),
            const((None, 1, MLSTM_WIDTH), (layer, 0, 0)),
            const((None, CONV_K, CONV_WIDTH), (layer, 0, 0)),
            const((None, N_BRANCH, BRANCH_WIDTH, d), (layer, 0, 0, 0)),
            const((None, d, d), (layer, 0, 0)),
            const((None, None, 1, d), (layer, 1, 0, 0)),
            const((None, None, 1, d), (layer, 1, 0, 0)),
        ],
        out_specs=pl.BlockSpec((1, tm, d), lambda b, s, *_: (b, s, 0)),
        scratch_shapes=[
            pltpu.VMEM((ATTN_BLOCK, LANES), BF16),
            pltpu.VMEM((ATTN_BLOCK, LANES), BF16),
            pltpu.VMEM((MLSTM_HEADS, MLSTM_HEAD_DIM, 2 * MLSTM_HEAD_DIM), F32),
            pltpu.VMEM((MLSTM_HEADS, SUBLANES, LANES), F32),
            pltpu.VMEM((tm + 2 * SUBLANES, CONV_WIDTH), F32),
            pltpu.VMEM((tm, N_BRANCH * BRANCH_WIDTH), BF16),
        ],
    )
    return pl.pallas_call(
        functools.partial(_mixer_kernel, tm=tm, tsub=tsub),
        out_shape=jax.ShapeDtypeStruct((bsz, seq, d), F32),
        grid_spec=grid_spec,
        compiler_params=pltpu.CompilerParams(
            dimension_semantics=("arbitrary", "arbitrary"),
            vmem_limit_bytes=VMEM_LIMIT_BYTES),
        name="token_mixer",
    )(sinks, x, pos, freq, expand, win, bin_, norm_w, conv_w, wb, wo, ln_w, ln_b)


def kernel(x, positions, w_in, b_in, attn_sinks, mlstm_norm_w, conv_w, w_branch, w_out,
           ffn1_w_gu, ffn1_w_down, ffn2_w_gu, ffn2_w_down, ln_w, ln_b):
    bsz, seq, d = x.shape
    depth = w_in.shape[0]
    ffn_w = ((ffn1_w_gu, ffn1_w_down), (ffn2_w_gu, ffn2_w_down))
    win = _pack_in_proj_weights(w_in)
    bin_ = _pack_in_proj(b_in)[:, None, :]
    wb = _pack_branch(w_branch.astype(BF16))
    wo = w_out.astype(BF16)
    norm_w = mlstm_norm_w[:, None, :]
    ln_w4 = ln_w[:, :, None, :]
    ln_b4 = ln_b[:, :, None, :]
    freq, expand = _rope_constants()
    for l in range(depth):
        x = _ffn_layer(x.reshape(bsz * seq, d), l, ffn_w[0][0], ffn_w[0][1], ln_w4, ln_b4, 0
                       ).reshape(bsz, seq, d)
        x = _mixer_layer(x, l, positions, freq, expand, win, bin_, attn_sinks[l], norm_w, conv_w, wb, wo,
                         ln_w4, ln_b4)
        x = _ffn_layer(x.reshape(bsz * seq, d), l, ffn_w[1][0], ffn_w[1][1], ln_w4, ln_b4, 2
                       ).reshape(bsz, seq, d)
    return x
```

```python
import functools
import math

import numpy as np
import jax
import jax.numpy as jnp
from jax import lax
from jax.experimental import pallas as pl
from jax.experimental.pallas import tpu as pltpu

D_MODEL = 1024
DEPTH = 2

HEAD_DIM = 64
N_Q_HEADS = 8
N_KV_HEADS = 2
GROUP = N_Q_HEADS // N_KV_HEADS
WINDOW = 128
ATTN_BLOCK = 128
ROPE_THETA = 500000.0
ROT_DIM = HEAD_DIM // 4
ATTN_WIDTH = N_Q_HEADS * HEAD_DIM
KV_WIDTH = N_KV_HEADS * HEAD_DIM

MLSTM_HEADS = 4
MLSTM_HEAD_DIM = D_MODEL // 8
MLSTM_WIDTH = MLSTM_HEADS * MLSTM_HEAD_DIM
MLSTM_CHUNK = 128

CONV_WIDTH = D_MODEL // 2
CONV_K = 3

N_BRANCH = 3
BRANCH_WIDTH = 512
D_FF = 2816
LN_EPS = 1e-5
ALPHA = (2.0 * DEPTH) ** 0.25
NEG_INF = -1e30

SIZES = (ATTN_WIDTH, KV_WIDTH, KV_WIDTH,
         MLSTM_WIDTH, MLSTM_WIDTH, MLSTM_WIDTH, MLSTM_WIDTH, MLSTM_HEADS, MLSTM_HEADS,
         CONV_WIDTH, CONV_WIDTH, CONV_WIDTH,
         N_BRANCH * D_MODEL)

LANES = 128
SUBLANES = 8
VMEM_LIMIT_BYTES = 56 * 1024 * 1024

OFF_AQ = 0
OFF_AK = OFF_AQ + ATTN_WIDTH
OFF_AV = OFF_AK + KV_WIDTH
OFF_M = OFF_AV + KV_WIDTH
OFF_C = OFF_M + 4 * MLSTM_WIDTH
OFF_G = OFF_C + 3 * CONV_WIDTH
OFF_IF = OFF_G + N_BRANCH * D_MODEL
P_PACKED = OFF_IF + 2 * LANES

Q_HEAD_ORDER = (0, 4, 1, 5, 2, 6, 3, 7)

BF16 = jnp.bfloat16
F32 = jnp.float32


def _dot(a, b):
    return jnp.dot(a, b, preferred_element_type=F32)


def _dot_nt(a, b):
    return lax.dot_general(a, b, (((1,), (1,)), ((), ())), preferred_element_type=F32)


def _dot_tn(a, b):
    return lax.dot_general(a, b, (((0,), (0,)), ((), ())), preferred_element_type=F32)


def _layer_norm_rows(y, w, b):
    mu = jnp.mean(y, axis=-1, keepdims=True)
    yc = y - mu
    var = jnp.mean(yc * yc, axis=-1, keepdims=True)
    return yc * lax.rsqrt(var + LN_EPS) * w + b


def _sigmoid(x):
    return 0.5 * jnp.tanh(0.5 * x) + 0.5


def _log_sigmoid(x):
    return jnp.minimum(x, 0.0) - jnp.log1p(jnp.exp(-jnp.abs(x)))


def _ffn_kernel(x_ref, wgu_hbm, wd_hbm, lnw_ref, lnb_ref, o_ref,
                wgu_ref, wd_ref, stage_gu_ref, stage_d_ref, sem, *, chunks, layer):
    i = pl.program_id(0)
    d_ff = wd_ref.shape[0]

    def chunk_copies(k, slot):
        c0, c1 = chunks[k]
        return (
            pltpu.make_async_copy(wgu_hbm.at[layer, :, c0:c1], stage_gu_ref.at[slot, 0], sem.at[slot, 0]),
            pltpu.make_async_copy(wgu_hbm.at[layer, :, d_ff + c0:d_ff + c1], stage_gu_ref.at[slot, 1],
                                  sem.at[slot, 1]),
            pltpu.make_async_copy(wd_hbm.at[layer, c0:c1, :], stage_d_ref.at[slot], sem.at[slot, 2]),
        )

    def run(load_weights):
        x = x_ref[...]
        xb = x.astype(BF16)
        acc = None
        if load_weights:
            for cp in chunk_copies(0, 0):
                cp.start()
        for k, (c0, c1) in enumerate(chunks):
            if load_weights:
                slot = k % 2
                if k + 1 < len(chunks):
                    for cp in chunk_copies(k + 1, 1 - slot):
                        cp.start()
                for cp in chunk_copies(k, slot):
                    cp.wait()
                wgu_ref[:, c0:c1] = stage_gu_ref[slot, 0].astype(BF16)
                wgu_ref[:, d_ff + c0:d_ff + c1] = stage_gu_ref[slot, 1].astype(BF16)
                wd_ref[c0:c1, :] = stage_d_ref[slot].astype(BF16)
            g = _dot(xb, wgu_ref[:, c0:c1])
            u = _dot(xb, wgu_ref[:, d_ff + c0:d_ff + c1])
            a = (g * _sigmoid(g) * u).astype(BF16)
            part = _dot(a, wd_ref[c0:c1, :])
            acc = part if acc is None else acc + part
        y = ALPHA * x + 0.5 * acc
        o_ref[...] = _layer_norm_rows(y, lnw_ref[...], lnb_ref[...])

    @pl.when(i == 0)
    def _first():
        run(True)

    @pl.when(i > 0)
    def _rest():
        run(False)


def _ffn_tiles(n_tokens, d_ff):
    tm = 512 if n_tokens % 512 == 0 else 128
    step = 2 * LANES
    assert d_ff % step == 0
    chunks = tuple((c, c + step) for c in range(0, d_ff, step))
    return tm, chunks


def _ffn_layer(x2d, layer, w_gu, w_down, ln_w, ln_b, ln_idx):
    n, d = x2d.shape
    d_ff = w_down.shape[1]
    tm, chunks = _ffn_tiles(n, d_ff)
    step = chunks[0][1] - chunks[0][0]

    def const(shape, idx):
        return pl.BlockSpec(shape, lambda i: idx, pipeline_mode=pl.Buffered(1))

    return pl.pallas_call(
        functools.partial(_ffn_kernel, chunks=chunks, layer=layer),
        out_shape=jax.ShapeDtypeStruct((n, d), F32),
        grid=(n // tm,),
        in_specs=[
            pl.BlockSpec((tm, d), lambda i: (i, 0)),
            pl.BlockSpec(memory_space=pl.ANY),
            pl.BlockSpec(memory_space=pl.ANY),
            const((None, None, 1, d), (layer, ln_idx, 0, 0)),
            const((None, None, 1, d), (layer, ln_idx, 0, 0)),
        ],
        out_specs=pl.BlockSpec((tm, d), lambda i: (i, 0)),
        scratch_shapes=[
            pltpu.VMEM((d, 2 * d_ff), BF16),
            pltpu.VMEM((d_ff, d), BF16),
            pltpu.VMEM((2, 2, d, step), F32),
            pltpu.VMEM((2, step, d), F32),
            pltpu.SemaphoreType.DMA((2, 3)),
        ],
        compiler_params=pltpu.CompilerParams(
            dimension_semantics=("arbitrary",),
            vmem_limit_bytes=VMEM_LIMIT_BYTES),
        name="swiglu_ln",
    )(x2d, w_gu, w_down, ln_w, ln_b)


def _rope_tables(pos_row, freq, expand):
    ang = freq * pos_row.astype(F32)
    pieces = []
    for t in (jnp.cos(ang), jnp.sin(ang)):
        hi = t.astype(BF16)
        r1 = t - hi.astype(F32)
        mid = r1.astype(BF16)
        lo = (r1 - mid.astype(F32)).astype(BF16)
        pieces += [hi, mid, lo]
    pieces += [jnp.ones((SUBLANES, LANES), BF16), jnp.zeros((SUBLANES, LANES), BF16)]
    cs = _dot_tn(jnp.concatenate(pieces, axis=0), expand)
    return cs[:, 0:LANES], cs[:, LANES:]


def _rope(t, cos_f, sin_f):
    half = ROT_DIM // 2
    lane = lax.broadcasted_iota(jnp.int32, (1, LANES), 1)
    up = pltpu.roll(t, LANES - half, axis=1)
    down = pltpu.roll(t, half, axis=1)
    return t * cos_f + jnp.where((lane & (HEAD_DIM - 1)) < half, up, down) * sin_f


def _split_kv(prev, cur):
    lane = lax.broadcasted_iota(jnp.int32, (1, LANES), 1)
    half0 = lane < HEAD_DIM
    zero = jnp.zeros((), BF16)
    both = jnp.concatenate([prev, cur], axis=0)
    return jnp.concatenate([jnp.where(half0, both, zero),
                            jnp.where(half0, zero, both)], axis=0)


def _attention_scores(q_all, k_cur, k_prev):
    return _dot_nt(q_all, _split_kv(k_prev, k_cur))


def _attention_softmax(s_all, thr, sinks_ref):
    L = ATTN_BLOCK
    lane = lax.broadcasted_iota(jnp.int32, (1, LANES), 1)
    half0 = lane < HEAD_DIM
    qi = lax.broadcasted_iota(jnp.int32, (L, 2 * L), 0)
    kj = lax.broadcasted_iota(jnp.int32, (L, 2 * L), 1)
    rel = qi + L - kj
    mask = (rel >= 0) & (rel < WINDOW) & (kj >= thr)
    ps, scales = [], []
    for c in range(N_Q_HEADS // 2):
        pg, inv = [], []
        for g in range(N_KV_HEADS):
            sink = sinks_ref[Q_HEAD_ORDER[2 * c + g]]
            sg = jnp.where(mask, s_all[c * L:(c + 1) * L, g * 2 * L:(g + 1) * 2 * L], NEG_INF)
            m = jnp.maximum(jnp.max(sg, axis=-1, keepdims=True), sink)
            p = jnp.exp(sg - m)
            den = jnp.sum(p, axis=-1, keepdims=True) + jnp.exp(sink - m)
            pg.append(p.astype(BF16))
            inv.append(1.0 / den)
        ps.append(jnp.concatenate(pg, axis=1))
        scales.append(jnp.where(half0, inv[0], inv[1]))
    return jnp.concatenate(ps, axis=0), jnp.concatenate(scales, axis=0)


def _mlstm_gates(gif, row0):
    L = MLSTM_CHUNK
    rows = slice(row0, row0 + L)
    lane = lax.broadcasted_iota(jnp.int32, (1, LANES), 1)
    gi = gif[rows, 0:LANES]
    gf = gif[rows, LANES:2 * LANES]
    lf = jnp.where(lane < 2 * MLSTM_HEADS, _log_sigmoid(gf), 0.0)
    r_i = lax.broadcasted_iota(jnp.int32, (L, L), 0)
    c_i = lax.broadcasted_iota(jnp.int32, (L, L), 1)
    tril = jnp.where(r_i >= c_i, 1.0, 0.0).astype(BF16)
    hi = lf.astype(BF16)
    r1 = lf - hi.astype(F32)
    mid = r1.astype(BF16)
    lo = (r1 - mid.astype(F32)).astype(BF16)
    bsum = _dot(tril, jnp.concatenate([hi, mid, lo], axis=1))
    bsum = bsum[:, 0:LANES] + bsum[:, LANES:2 * LANES] + bsum[:, 2 * LANES:]
    t_col = jnp.where(lane < MLSTM_HEADS, gi - bsum, bsum)
    return t_col, t_col.T


def _mlstm_stage1(mqk, t_col, t_row, s_ref, m_ref, row0):
    L = MLSTM_CHUNK
    dh = MLSTM_HEAD_DIM
    rows = slice(row0, row0 + L)
    r_i = lax.broadcasted_iota(jnp.int32, (L, L), 0)
    c_i = lax.broadcasted_iota(jnp.int32, (L, L), 1)
    causal = r_i >= c_i
    scale = dh ** -0.5
    heads = []
    for h in range(MLSTM_HEADS):
        q = mqk[rows, h * dh:(h + 1) * dh].astype(BF16)
        k = mqk[rows, MLSTM_WIDTH + h * dh:MLSTM_WIDTH + (h + 1) * dh] * scale
        state = s_ref[h]
        qk = _dot_nt(q, k.astype(BF16))
        from_state = _dot(q, state.astype(BF16))
        heads.append(dict(k=k, state=state, qk=qk, from_state=from_state))
    for h, hd in enumerate(heads):
        b_col = t_col[:, MLSTM_HEADS + h:MLSTM_HEADS + h + 1]
        d_col = t_col[:, h:h + 1]
        d_row = t_row[h:h + 1, :]
        m_prev = m_ref[h][0:1, 0:1]
        logw = jnp.where(causal, b_col + d_row, NEG_INF)
        inter = b_col + m_prev
        m_t = jnp.maximum(inter, jnp.max(logw, axis=-1, keepdims=True))
        scores = hd["qk"] * jnp.exp(logw - m_t)
        m_new = m_t[L - 1:L, :]
        b_last = b_col[L - 1:L, :]
        ws = jnp.exp(b_last + d_col - m_new)
        hd.update(m_t=m_t, w_inter=jnp.exp(inter - m_t), scores=scores,
                  row_sum=jnp.sum(scores, axis=-1, keepdims=True),
                  decay=jnp.exp(b_last + m_prev - m_new), m_new=m_new,
                  kw=(hd["k"] * ws).astype(BF16))
    return heads


def _mlstm_stage2(mvo, heads, s_ref, m_ref, nw_ref, row0):
    L = MLSTM_CHUNK
    dh = MLSTM_HEAD_DIM
    rows = slice(row0, row0 + L)
    lane = lax.broadcasted_iota(jnp.int32, (1, LANES), 1)
    ones_tile = jnp.broadcast_to(jnp.where(lane == 0, 1.0, 0.0).astype(BF16), (L, LANES))
    outs = []
    for h, hd in enumerate(heads):
        v = mvo[rows, h * dh:(h + 1) * dh].astype(BF16)
        hd["intra"] = _dot(hd["scores"].astype(BF16), v)
        v_ext = jnp.concatenate([v, ones_tile], axis=1)
        s_ref[h] = hd["decay"] * hd["state"] + _dot_tn(hd["kw"], v_ext)
        m_ref[h] = jnp.broadcast_to(hd["m_new"], (SUBLANES, LANES))
    for h, hd in enumerate(heads):
        og = mvo[rows, MLSTM_WIDTH + h * dh:MLSTM_WIDTH + (h + 1) * dh]
        num = hd["w_inter"] * hd["from_state"][:, 0:dh] + hd["intra"]
        den = hd["w_inter"] * hd["from_state"][:, dh:dh + 1] + hd["row_sum"]
        hh = num / jnp.maximum(jnp.abs(den), jnp.exp(-hd["m_t"]))
        mu = jnp.mean(hh, axis=-1, keepdims=True)
        hc = hh - mu
        hn = hc * lax.rsqrt(jnp.mean(hc * hc, axis=-1, keepdims=True) + LN_EPS)
        outs.append(hn * nw_ref[:, h * dh:(h + 1) * dh] * _sigmoid(og))
    return outs


class _SubTile:
    def __init__(self, sub, tsub, s_idx, refs):
        self.sub, self.tsub, self.s_idx, self.r = sub, tsub, s_idx, refs
        self.r0 = sub * tsub
        self.nblk = tsub // ATTN_BLOCK
        self.x = refs["x"][0, self.r0:self.r0 + tsub, :]
        self.xb = self.x.astype(BF16)
        self.gates = None
        self.merged = None

    def proj(self, lo, hi):
        return _dot(self.xb, self.r["win"][:, lo:hi]) + self.r["bin"][:, lo:hi]

    def ropes(self):
        r, n = self.r, self.nblk
        self.rope = [_rope_tables(r["pos"][0, self.sub * n + j:self.sub * n + j + 1, :],
                                  r["freq"][...], r["expand"][...]) for j in range(n)]

    def proj_head(self):
        p = self.proj(OFF_AQ, OFF_C)
        self.qkv = p[:, OFF_AQ:OFF_M]
        self.mqk = p[:, OFF_M:OFF_M + 2 * MLSTM_WIDTH]
        self.mvo = p[:, OFF_M + 2 * MLSTM_WIDTH:OFF_C]

    def proj_gates(self):
        gif = self.proj(OFF_IF, P_PACKED)
        self.mgates = [_mlstm_gates(gif, j * MLSTM_CHUNK) for j in range(self.nblk)]

    def conv(self):
        r, r0, tsub = self.r, self.r0, self.tsub
        cproj = self.proj(OFF_C, OFF_G)
        u = cproj[:, 2 * CONV_WIDTH:] * cproj[:, 0:CONV_WIDTH]
        r["conv"][SUBLANES + r0:SUBLANES + r0 + tsub, :] = u
        u1 = r["conv"][pl.ds(SUBLANES + r0 - 1, tsub), :]
        u2 = r["conv"][pl.ds(SUBLANES + r0 - 2, tsub), :]
        yc = r["cw"][0:1, :] * u2 + r["cw"][1:2, :] * u1 + r["cw"][2:3, :] * u
        r["y"][r0:r0 + tsub, 2 * BRANCH_WIDTH:] = (cproj[:, CONV_WIDTH:2 * CONV_WIDTH] * yc).astype(BF16)

    def branch_gates(self):
        g = _sigmoid(self.proj(OFF_G, OFF_IF))
        self.gates = [g[:, b * D_MODEL:(b + 1) * D_MODEL] for b in range(N_BRANCH)]

    def stage1(self, j, kv):
        L = ATTN_BLOCK
        rows = slice(j * L, (j + 1) * L)
        cos_f, sin_f = self.rope[j]
        q_all = jnp.concatenate(
            [(_rope(self.qkv[rows, c * LANES:(c + 1) * LANES], cos_f, sin_f)
              * (HEAD_DIM ** -0.5)).astype(BF16) for c in range(N_Q_HEADS // 2)], axis=0)
        self.k_cur = _rope(self.qkv[rows, OFF_AK:OFF_AK + LANES], cos_f, sin_f).astype(BF16)
        self.v_cur = self.qkv[rows, OFF_AV:OFF_AV + LANES].astype(BF16)
        thr = jnp.where(self.s_idx == 0, L, 0) if (self.sub == 0 and j == 0) else 0
        s_all = _attention_scores(q_all, self.k_cur, kv[0])
        self.heads = _mlstm_stage1(self.mqk, self.mgates[j][0], self.mgates[j][1],
                                   self.r["s"], self.r["m"], j * L)
        self.p_all, self.o_scale = _attention_softmax(s_all, thr, self.r["sinks"])

    def stage2(self, j, kv):
        L = ATTN_BLOCK
        r = self.r
        yrows = slice(self.r0 + j * L, self.r0 + (j + 1) * L)
        o_all = _dot(self.p_all, _split_kv(kv[1], self.v_cur)) * self.o_scale
        for c in range(N_Q_HEADS // 2):
            r["y"][yrows, c * LANES:(c + 1) * LANES] = o_all[c * L:(c + 1) * L, :].astype(BF16)
        outs = _mlstm_stage2(self.mvo, self.heads, r["s"], r["m"], r["nw"], j * L)
        for h, o in enumerate(outs):
            r["y"][yrows, BRANCH_WIDTH + h * MLSTM_HEAD_DIM:
                   BRANCH_WIDTH + (h + 1) * MLSTM_HEAD_DIM] = o.astype(BF16)
        return self.k_cur, self.v_cur

    def branch(self, b):
        r, r0, tsub = self.r, self.r0, self.tsub
        term = self.gates[b] * _dot(
            r["y"][r0:r0 + tsub, b * BRANCH_WIDTH:(b + 1) * BRANCH_WIDTH], r["wb"][b])
        self.merged = term if self.merged is None else self.merged + term

    def finish(self):
        r = self.r
        mix = _dot(self.merged.astype(BF16), r["wo"][...])
        y = ALPHA * self.x + mix
        r["o"][0, self.r0:self.r0 + self.tsub, :] = _layer_norm_rows(y, r["lnw"][...], r["lnb"][...])


def _mixer_kernel(sinks_ref, x_ref, pos_ref, freq_ref, expand_ref, win_ref, bin_ref, nw_ref, cw_ref,
                  wb_ref, wo_ref, lnw_ref, lnb_ref, o_ref,
                  kprev_ref, vprev_ref, s_ref, m_ref, conv_ref, y_ref, *, tm, tsub):
    s_idx = pl.program_id(1)

    @pl.when(s_idx == 0)
    def _reset():
        kprev_ref[...] = jnp.zeros_like(kprev_ref)
        vprev_ref[...] = jnp.zeros_like(vprev_ref)
        s_ref[...] = jnp.zeros_like(s_ref)
        m_ref[...] = jnp.zeros_like(m_ref)
        conv_ref[0:SUBLANES, :] = jnp.zeros((SUBLANES, CONV_WIDTH), F32)

    refs = dict(sinks=sinks_ref, x=x_ref, pos=pos_ref, freq=freq_ref, expand=expand_ref, win=win_ref,
                bin=bin_ref, nw=nw_ref, cw=cw_ref, wb=wb_ref, wo=wo_ref, lnw=lnw_ref, lnb=lnb_ref,
                o=o_ref, s=s_ref, m=m_ref, conv=conv_ref, y=y_ref)
    nsub = tm // tsub
    tiles = [_SubTile(t, tsub, s_idx, refs) for t in range(nsub)]

    def head_steps(t):
        return [t.ropes, t.proj_head, t.proj_gates]

    def tail_steps(t):
        return [lambda: t.branch(2), lambda: t.branch(0), lambda: t.branch(1), t.finish]

    for step in head_steps(tiles[0]):
        step()
    kv = (kprev_ref[...], vprev_ref[...])
    carry_tail = []
    for i, t in enumerate(tiles):
        nxt = head_steps(tiles[i + 1]) if i + 1 < nsub else []
        fillers = carry_tail + [t.conv] + nxt + ([] if i + 1 < nsub else [t.branch_gates])
        deferred = [t.branch_gates] if i + 1 < nsub else []
        slots = 2 * t.nblk
        for slot in range(slots):
            j = slot // 2
            if slot % 2 == 0:
                t.stage1(j, kv)
            else:
                kv = t.stage2(j, kv)
            take = -(-len(fillers) // (slots - slot))
            for step in fillers[:take]:
                step()
            fillers = fillers[take:]
        carry_tail = deferred + tail_steps(t)
    for step in carry_tail:
        step()
    kprev_ref[...], vprev_ref[...] = kv
    conv_ref[0:SUBLANES, :] = conv_ref[tm:tm + SUBLANES, :]


ROPE_K = 8 * SUBLANES


def _rope_constants():
    half = ROT_DIM // 2
    inv_freq = ROPE_THETA ** (-np.arange(0, ROT_DIM, 2, dtype=np.float64) / ROT_DIM)
    freq = np.repeat(inv_freq[:, None], LANES, axis=1).astype(np.float32)
    l64 = np.arange(LANES) % HEAD_DIM
    expand = np.zeros((ROPE_K, 2 * LANES), np.float32)
    for f in range(half):
        cos_lanes = np.where((l64 < ROT_DIM) & (l64 % half == f), 1.0, 0.0)
        sin_lanes = np.where(l64 == f, -1.0, 0.0) + np.where(l64 == f + half, 1.0, 0.0)
        for piece in range(3):
            expand[piece * SUBLANES + f, :LANES] = cos_lanes
            expand[(3 + piece) * SUBLANES + f, LANES:] = sin_lanes
    expand[6 * SUBLANES, :LANES] = np.where(l64 >= ROT_DIM, 1.0, 0.0)
    return jnp.asarray(freq), jnp.asarray(expand, dtype=BF16)


def _pack_in_proj(w):
    offs = np.concatenate([[0], np.cumsum(SIZES)]).tolist()
    seg = [w[..., offs[i]:offs[i + 1]] for i in range(len(SIZES))]
    (aq, ak, av, mq, mk, mv, mo, mi, mf, cx, cb, cc, gates) = seg
    aq_heads = [aq[..., h * HEAD_DIM:(h + 1) * HEAD_DIM] for h in Q_HEAD_ORDER]
    zi = jnp.zeros(w.shape[:-1] + (LANES - MLSTM_HEADS,), w.dtype)
    zf = jnp.zeros(w.shape[:-1] + (LANES - 2 * MLSTM_HEADS,), w.dtype)
    return jnp.concatenate(aq_heads + [ak, av, mq, mk, mv, mo, cx, cb, cc, gates,
                                       mi, zi, mf, mf, zf], axis=-1)


def _pack_kernel(wt_ref, o_ref):
    offs = np.concatenate([[0], np.cumsum(SIZES)]).tolist()
    lane = lax.broadcasted_iota(jnp.int32, (1, LANES), 1)

    def put(lo, src):
        o_ref[:, lo:lo + src.shape[0]] = src.T.astype(BF16)

    for c in range(N_Q_HEADS // 2):
        pair = [wt_ref[h * HEAD_DIM:(h + 1) * HEAD_DIM, :] for h in Q_HEAD_ORDER[2 * c:2 * c + 2]]
        put(c * LANES, jnp.concatenate(pair, axis=0))
    for lo in range(OFF_AK, OFF_C, 4 * LANES):
        hi = min(lo + 4 * LANES, OFF_C)
        put(lo, wt_ref[lo:hi, :])
    shift = offs[9] - OFF_C
    for lo in range(OFF_C, OFF_IF, 4 * LANES):
        hi = min(lo + 4 * LANES, OFF_IF)
        put(lo, wt_ref[lo + shift:hi + shift, :])
    t = wt_ref[offs[7]:offs[7] + LANES, :].T
    o_ref[:, OFF_IF:OFF_IF + LANES] = jnp.where(lane < MLSTM_HEADS, t, 0.0).astype(BF16)
    f_lo = pltpu.roll(t, LANES - MLSTM_HEADS, axis=1)
    f_tile = jnp.where(lane < MLSTM_HEADS, f_lo, jnp.where(lane < 2 * MLSTM_HEADS, t, 0.0))
    o_ref[:, OFF_IF + LANES:P_PACKED] = f_tile.astype(BF16)


def _pack_in_proj_weights(w_in):
    depth, d, p_in = w_in.shape
    rows = 256
    return pl.pallas_call(
        _pack_kernel,
        out_shape=jax.ShapeDtypeStruct((depth, d, P_PACKED), BF16),
        grid=(depth, d // rows),
        in_specs=[pl.BlockSpec((None, p_in, rows), lambda l, i: (l, 0, i))],
        out_specs=pl.BlockSpec((None, rows, P_PACKED), lambda l, i: (l, i, 0)),
        compiler_params=pltpu.CompilerParams(
            dimension_semantics=("arbitrary", "arbitrary"),
            vmem_limit_bytes=VMEM_LIMIT_BYTES),
        name="pack_in_proj",
    )(jnp.swapaxes(w_in, 1, 2))


def _pack_branch(w_branch):
    wb0 = jnp.concatenate([w_branch[:, 0:1, h * HEAD_DIM:(h + 1) * HEAD_DIM, :] for h in Q_HEAD_ORDER],
                          axis=2)
    return jnp.concatenate([wb0, w_branch[:, 1:]], axis=1)


def _mixer_tile(seq):
    if seq % 512 == 0:
        return 512, 256
    if seq % 256 == 0:
        return 256, 256
    return ATTN_BLOCK, ATTN_BLOCK


def _mixer_layer(x, layer, positions, freq, expand, win, bin_, sinks, norm_w, conv_w, wb, wo, ln_w, ln_b):
    bsz, seq, d = x.shape
    tm, tsub = _mixer_tile(seq)
    ns = seq // tm
    nblk = tm // ATTN_BLOCK
    pos = positions.reshape(bsz * ns, nblk, ATTN_BLOCK)

    def const(shape, idx):
        return pl.BlockSpec(shape, lambda b, s, *_: idx, pipeline_mode=pl.Buffered(1))

    grid_spec = pltpu.PrefetchScalarGridSpec(
        num_scalar_prefetch=1,
        grid=(bsz, ns),
        in_specs=[
            pl.BlockSpec((1, tm, d), lambda b, s, *_: (b, s, 0)),
            pl.BlockSpec((1, nblk, ATTN_BLOCK), lambda b, s, *_: (b * ns + s, 0, 0)),
            const((SUBLANES, LANES), (0, 0)),
            const((ROPE_K, 2 * LANES), (0, 0)),
            const((None, d, P_PACKED), (layer, 0, 0)),
            const((None, 1, P_PACKED), (layer, 0, 0)),
            const((None, 1, MLSTM_WIDTH), (layer, 0, 0)),
            const((None, CONV_K, CONV_WIDTH), (layer, 0, 0)),
            const((None, N_BRANCH, BRANCH_WIDTH, d), (layer, 0, 0, 0)),
            const((None, d, d), (layer, 0, 0)),
            const((None, None, 1, d), (layer, 1, 0, 0)),
            const((None, None, 1, d), (layer, 1, 0, 0)),
        ],
        out_specs=pl.BlockSpec((1, tm, d), lambda b, s, *_: (b, s, 0)),
        scratch_shapes=[
            pltpu.VMEM((ATTN_BLOCK, LANES), BF16),
            pltpu.VMEM((ATTN_BLOCK, LANES), BF16),
            pltpu.VMEM((MLSTM_HEADS, MLSTM_HEAD_DIM, 2 * MLSTM_HEAD_DIM), F32),
            pltpu.VMEM((MLSTM_HEADS, SUBLANES, LANES), F32),
            pltpu.VMEM((tm + 2 * SUBLANES, CONV_WIDTH), F32),
            pltpu.VMEM((tm, N_BRANCH * BRANCH_WIDTH), BF16),
        ],
    )
    return pl.pallas_call(
        functools.partial(_mixer_kernel, tm=tm, tsub=tsub),
        out_shape=jax.ShapeDtypeStruct((bsz, seq, d), F32),
        grid_spec=grid_spec,
        compiler_params=pltpu.CompilerParams(
            dimension_semantics=("arbitrary", "arbitrary"),
            vmem_limit_bytes=VMEM_LIMIT_BYTES),
        name="token_mixer",
    )(sinks, x, pos, freq, expand, win, bin_, norm_w, conv_w, wb, wo, ln_w, ln_b)


def kernel(x, positions, w_in, b_in, attn_sinks, mlstm_norm_w, conv_w, w_branch, w_out,
           ffn1_w_gu, ffn1_w_down, ffn2_w_gu, ffn2_w_down, ln_w, ln_b):
    bsz, seq, d = x.shape
    depth = w_in.shape[0]
    ffn_w = ((ffn1_w_gu, ffn1_w_down), (ffn2_w_gu, ffn2_w_down))
    win = _pack_in_proj_weights(w_in)
    bin_ = _pack_in_proj(b_in)[:, None, :]
    wb = _pack_branch(w_branch.astype(BF16))
    wo = w_out.astype(BF16)
    norm_w = mlstm_norm_w[:, None, :]
    ln_w4 = ln_w[:, :, None, :]
    ln_b4 = ln_b[:, :, None, :]
    freq, expand = _rope_constants()
    for l in range(depth):
        x = _ffn_layer(x.reshape(bsz * seq, d), l, ffn_w[0][0], ffn_w[0][1], ln_w4, ln_b4, 0
                       ).reshape(bsz, seq, d)
        x = _mixer_layer(x, l, positions, freq, expand, win, bin_, attn_sinks[l], norm_w, conv_w, wb, wo,
                         ln_w4, ln_b4)
        x = _ffn_layer(x.reshape(bsz * seq, d), l, ffn_w[1][0], ffn_w[1][1], ln_w4, ln_b4, 2
                       ).reshape(bsz, seq, d)
    return x
```
